```python
import math
import jax, jax.numpy as jnp
from jax import lax
import numpy as np

D_MODEL = 2048
BATCH = 4
SEQ = 4096
DEPTH = 1
DEC_BATCH = 32
DEC_SEQ = 16
PAST_LEN = 4096

CHUNK = 64
HEAD_DIM = 128
GDN_HEADS = 8
GDN_CONV = 4
SWA_HEADS = 8
SWA_KV_HEADS = 2
SWA_GROUP = SWA_HEADS // SWA_KV_HEADS
WINDOW = 128
D_FF = 5632
FFN_CONV = 3
N_MOD = 6
EPS = 1e-6

GDN_WIDTH = GDN_HEADS * HEAD_DIM
SWA_WIDTH = SWA_HEADS * HEAD_DIM
KV_WIDTH = SWA_KV_HEADS * HEAD_DIM
QKV_D_WIDTH = 3 * GDN_WIDTH
MIX_WIDTH = GDN_WIDTH + SWA_WIDTH
IN_WIDTH = QKV_D_WIDTH + GDN_WIDTH + 2 * GDN_HEADS + SWA_WIDTH + 2 * KV_WIDTH

kernel_name = 'hybrid_gdn_swa_convffn_stream_step'


def rms_norm(x, w):
    xf = x.astype(jnp.float32)
    y = xf * lax.rsqrt(jnp.mean(xf * xf, axis=-1, keepdims=True) + EPS)
    return (y * w.astype(jnp.float32)).astype(x.dtype)


def l2_norm(x):
    xf = x.astype(jnp.float32)
    return xf * lax.rsqrt(jnp.sum(xf * xf, axis=-1, keepdims=True) + EPS)


def causal_dwconv(x_ext, w):
    width = w.shape[0]
    L = x_ext.shape[1] - width + 1
    out = x_ext[:, 0:L] * w[0]
    for i in range(1, width):
        out = out + x_ext[:, i:i + L] * w[i]
    return out


def alibi_slopes():
    h = jnp.arange(1, SWA_HEADS + 1, dtype=jnp.float32)
    return (2.0 ** (-8.0 * h / SWA_HEADS)).reshape(SWA_KV_HEADS, SWA_GROUP)


def gated_delta_rule(q, k, v, g, beta, s0, chunk):
    B, L, H, DK = q.shape
    DV = v.shape[-1]
    n = L // chunk

    def to_blocks(t):
        t = t.reshape((B, n, chunk, H) + t.shape[3:])
        return jnp.moveaxis(t, 3, 1)

    q, k, v, g, beta = [to_blocks(t) for t in (q, k, v, g, beta)]
    gc = jnp.cumsum(g, axis=-1)
    idx = jnp.arange(chunk)
    incl = idx[:, None] >= idx[None, :]
    strict = idx[:, None] > idx[None, :]
    decay = jnp.exp(jnp.where(incl, gc[..., :, None] - gc[..., None, :], -jnp.inf))
    kb = k * beta[..., None]
    a = jnp.where(strict, jnp.einsum('bhnid,bhnjd->bhnij', kb, k) * decay, 0.0)
    t_mat = a + jnp.eye(chunk, dtype=a.dtype)
    rhs = jnp.concatenate([v * beta[..., None], kb * jnp.exp(gc)[..., None]], axis=-1)
    sol = lax.linalg.triangular_solve(t_mat, rhs, left_side=True, lower=True, unit_diagonal=True)
    u, w = sol[..., :DV], sol[..., DV:]
    qk = jnp.where(incl, jnp.einsum('bhnid,bhnjd->bhnij', q, k) * decay, 0.0)
    qg = q * jnp.exp(gc)[..., None]
    kd = k * jnp.exp(gc[..., -1:] - gc)[..., None]
    gl = jnp.exp(gc[..., -1])

    def step(s, xs):
        u_n, w_n, qk_n, qg_n, kd_n, gl_n = xs
        v_new = u_n - jnp.einsum('bhck,bhkv->bhcv', w_n, s)
        o_n = jnp.einsum('bhck,bhkv->bhcv', qg_n, s) + jnp.einsum('bhij,bhjv->bhiv', qk_n, v_new)
        s = s * gl_n[..., None, None] + jnp.einsum('bhck,bhcv->bhkv', kd_n, v_new)
        return s, o_n

    xs = tuple(jnp.moveaxis(t, 2, 0) for t in (u, w, qk, qg, kd, gl))
    s_fin, o = lax.scan(step, s0, xs)
    o = jnp.transpose(o, (1, 0, 3, 2, 4)).reshape(B, L, H, DV)
    return o, s_fin


def banded_sink_alibi_attention(q, k_ext, v_ext, pos0, chunk, sinks, slopes):
    B, L, HQ, D = q.shape
    n = L // chunk
    span = WINDOW + chunk
    kidx = jnp.arange(n)[:, None] * chunk + jnp.arange(span)[None, :]
    kb = jnp.take(k_ext, kidx, axis=1)
    vb = jnp.take(v_ext, kidx, axis=1)
    qb = q.reshape(B, n, chunk, SWA_KV_HEADS, SWA_GROUP, D)
    qpos = pos0 + jnp.arange(L).reshape(n, chunk)
    kpos = pos0 - WINDOW + kidx
    dist = jnp.abs(qpos[:, :, None] - kpos[:, None, :]).astype(jnp.float32)
    valid = (kpos >= 0)[:, None, :]
    s = jnp.einsum('bnqhgd,bnkhd->bnhgqk', qb, kb).astype(jnp.float32) * (D ** -0.5)
    s = s - slopes[:, :, None, None] * dist[:, None, None]
    s = jnp.where(valid[:, None, None], s, -jnp.inf)
    sink = sinks.astype(jnp.float32)[:, :, None, None]
    m = jnp.maximum(jnp.max(s, axis=-1, keepdims=True), sink)
    p = jnp.exp(s - m)
    probs = p / (jnp.sum(p, axis=-1, keepdims=True) + jnp.exp(sink - m))
    o = jnp.einsum('bnhgqk,bnkhd->bnqhgd', probs.astype(v_ext.dtype), vb)
    return o.reshape(B, L, HQ * D)


def hybrid_layer(x, c, conv_prev, s0, k_prev, v_prev, ffn_prev, pos0,
                 ada_w, ada_b, norm1_w, norm2_w, w_in, conv_qkv_w, a_log, dt_bias, gdn_norm_w,
                 q_norm_w, k_norm_w, sinks, w_o, w_up, ffn_conv_w, ffn_conv_b, w_down):
    B, L, _ = x.shape
    chunk = min(CHUNK, L)
    f32 = jnp.float32
    mod = (jax.nn.silu(c) @ ada_w + ada_b).reshape(B, N_MOD, D_MODEL)[:, :, None, :]
    shift1, scale1, gate1, shift2, scale2, gate2 = [mod[:, i] for i in range(N_MOD)]

    h = rms_norm(x, norm1_w) * (1 + scale1) + shift1
    proj = h @ w_in
    cuts = [QKV_D_WIDTH, QKV_D_WIDTH + GDN_WIDTH, QKV_D_WIDTH + GDN_WIDTH + GDN_HEADS,
            QKV_D_WIDTH + GDN_WIDTH + 2 * GDN_HEADS, QKV_D_WIDTH + GDN_WIDTH + 2 * GDN_HEADS + SWA_WIDTH,
            QKV_D_WIDTH + GDN_WIDTH + 2 * GDN_HEADS + SWA_WIDTH + KV_WIDTH]
    qkv_d, z_d, a_d, b_d, q_a, k_a, v_a = jnp.split(proj, cuts, axis=-1)

    qkv_ext = jnp.concatenate([conv_prev, qkv_d], axis=1)
    qkv_c = jax.nn.silu(causal_dwconv(qkv_ext, conv_qkv_w))
    qd, kd, vd = jnp.split(qkv_c, 3, axis=-1)
    qd = l2_norm(qd.reshape(B, L, GDN_HEADS, HEAD_DIM)) * (HEAD_DIM ** -0.5)
    kd = l2_norm(kd.reshape(B, L, GDN_HEADS, HEAD_DIM))
    vd = vd.reshape(B, L, GDN_HEADS, HEAD_DIM).astype(f32)
    g = -jnp.exp(a_log.astype(f32)) * jax.nn.softplus(a_d.astype(f32) + dt_bias.astype(f32))
    beta = jax.nn.sigmoid(b_d.astype(f32))
    o_d, s_new = gated_delta_rule(qd, kd, vd, g, beta, s0.astype(f32), chunk)
    o_d = rms_norm(o_d, gdn_norm_w) * jax.nn.silu(z_d.reshape(B, L, GDN_HEADS, HEAD_DIM).astype(f32))
    o_d = o_d.reshape(B, L, GDN_WIDTH).astype(x.dtype)

    qa = rms_norm(q_a.reshape(B, L, SWA_HEADS, HEAD_DIM), q_norm_w)
    ka = rms_norm(k_a.reshape(B, L, SWA_KV_HEADS, HEAD_DIM), k_norm_w)
    va = v_a.reshape(B, L, SWA_KV_HEADS, HEAD_DIM)
    k_ext = jnp.concatenate([k_prev, ka], axis=1)
    v_ext = jnp.concatenate([v_prev, va], axis=1)
    o_a = banded_sink_alibi_attention(qa, k_ext, v_ext, pos0, chunk,
                                      sinks.reshape(SWA_KV_HEADS, SWA_GROUP), alibi_slopes())

    x = x + gate1 * (jnp.concatenate([o_d, o_a], axis=-1) @ w_o)

    h2 = rms_norm(x, norm2_w) * (1 + scale2) + shift2
    u = h2 @ w_up
    u_ext = jnp.concatenate([ffn_prev, u], axis=1)
    u_c = causal_dwconv(u_ext, ffn_conv_w) + ffn_conv_b
    gt, up = jnp.split(u_c, 2, axis=-1)
    x = x + gate2 * ((jax.nn.silu(gt) * up) @ w_down)
    return (x, qkv_ext[:, -(GDN_CONV - 1):], s_new.astype(s0.dtype), k_ext[:, -WINDOW:],
            v_ext[:, -WINDOW:], u_ext[:, -(FFN_CONV - 1):])


def setup_inputs(seed: int = 0) -> dict:
    key = jax.random.key(seed)
    kit = iter(jax.random.split(key, 40))
    f32 = jnp.float32

    def nrm(shape, scale=1.0):
        return jax.random.normal(next(kit), shape, f32) * scale

    dt = jnp.exp(jax.random.uniform(next(kit), (DEPTH, GDN_HEADS), f32, math.log(1e-3), math.log(1e-1)))
    return {
        'x_prompt': nrm((BATCH, SEQ, D_MODEL)),
        'x_sample': nrm((DEC_BATCH, DEC_SEQ, D_MODEL)),
        'state_conv_qkv': nrm((DEPTH, DEC_BATCH, GDN_CONV - 1, QKV_D_WIDTH)),
        'state_delta': nrm((DEPTH, DEC_BATCH, GDN_HEADS, HEAD_DIM, HEAD_DIM), 0.1),
        'cache_swa_k': nrm((DEPTH, DEC_BATCH, WINDOW, SWA_KV_HEADS, HEAD_DIM)),
        'cache_swa_v': nrm((DEPTH, DEC_BATCH, WINDOW, SWA_KV_HEADS, HEAD_DIM)),
        'state_ffn_conv': nrm((DEPTH, DEC_BATCH, FFN_CONV - 1, 2 * D_FF)),
        'c_prompt': nrm((BATCH, D_MODEL)),
        'c_sample': nrm((DEC_BATCH, D_MODEL)),
        'ada_w': nrm((DEPTH, D_MODEL, N_MOD * D_MODEL), 0.5 * D_MODEL ** -0.5),
        'ada_b': nrm((DEPTH, N_MOD * D_MODEL), 0.1),
        'norm1_w': 1.0 + nrm((DEPTH, D_MODEL), 0.05),
        'norm2_w': 1.0 + nrm((DEPTH, D_MODEL), 0.05),
        'w_in': nrm((DEPTH, D_MODEL, IN_WIDTH), D_MODEL ** -0.5),
        'conv_qkv_w': nrm((DEPTH, GDN_CONV, QKV_D_WIDTH), GDN_CONV ** -0.5),
        'a_log': jnp.log(jax.random.uniform(next(kit), (DEPTH, GDN_HEADS), f32, 1.0, 16.0)),
        'dt_bias': dt + jnp.log(-jnp.expm1(-dt)),
        'gdn_norm_w': 1.0 + nrm((DEPTH, HEAD_DIM), 0.05),
        'q_norm_w': 1.0 + nrm((DEPTH, HEAD_DIM), 0.05),
        'k_norm_w': 1.0 + nrm((DEPTH, HEAD_DIM), 0.05),
        'sinks': nrm((DEPTH, SWA_HEADS), 0.5),
        'w_o': nrm((DEPTH, MIX_WIDTH, D_MODEL), MIX_WIDTH ** -0.5),
        'w_up': nrm((DEPTH, D_MODEL, 2 * D_FF), D_MODEL ** -0.5),
        'ffn_conv_w': nrm((DEPTH, FFN_CONV, 2 * D_FF), FFN_CONV ** -0.5),
        'ffn_conv_b': nrm((DEPTH, 2 * D_FF), 0.02),
        'w_down': nrm((DEPTH, D_FF, D_MODEL), D_FF ** -0.5),
    }


def reference(x_prompt, x_sample, state_conv_qkv, state_delta, cache_swa_k, cache_swa_v, state_ffn_conv,
              c_prompt, c_sample, ada_w, ada_b, norm1_w, norm2_w, w_in, conv_qkv_w, a_log, dt_bias,
              gdn_norm_w, q_norm_w, k_norm_w, sinks, w_o, w_up, ffn_conv_w, ffn_conv_b, w_down):
    xp, xs = x_prompt, x_sample
    acc_p = [[] for _ in range(5)]
    acc_s = [[] for _ in range(5)]
    for l in range(DEPTH):
        lw = (ada_w[l], ada_b[l], norm1_w[l], norm2_w[l], w_in[l], conv_qkv_w[l], a_log[l], dt_bias[l],
              gdn_norm_w[l], q_norm_w[l], k_norm_w[l], sinks[l], w_o[l], w_up[l], ffn_conv_w[l],
              ffn_conv_b[l], w_down[l])
        b, dt = xp.shape[0], xp.dtype
        zero_state = (jnp.zeros((b, GDN_CONV - 1, QKV_D_WIDTH), dt),
                      jnp.zeros((b, GDN_HEADS, HEAD_DIM, HEAD_DIM), dt),
                      jnp.zeros((b, WINDOW, SWA_KV_HEADS, HEAD_DIM), dt),
                      jnp.zeros((b, WINDOW, SWA_KV_HEADS, HEAD_DIM), dt),
                      jnp.zeros((b, FFN_CONV - 1, 2 * D_FF), dt))
        xp, *new_p = hybrid_layer(xp, c_prompt, *zero_state, 0, *lw)
        xs, *new_s = hybrid_layer(xs, c_sample, state_conv_qkv[l], state_delta[l], cache_swa_k[l],
                                  cache_swa_v[l], state_ffn_conv[l], PAST_LEN, *lw)
        for acc, t in zip(acc_p, new_p):
            acc.append(t)
        for acc, t in zip(acc_s, new_s):
            acc.append(t)
    p_conv_qkv, p_delta, p_swa_k, p_swa_v, p_ffn_conv = [jnp.stack(a) for a in acc_p]
    s_conv_qkv, s_delta, s_swa_k, s_swa_v, s_ffn_conv = [jnp.stack(a) for a in acc_s]
    return (xp, xs, p_conv_qkv, p_delta, p_swa_k, p_swa_v, p_ffn_conv,
            s_conv_qkv, s_delta, s_swa_k, s_swa_v, s_ffn_conv)
```

```python
import functools
import math

import jax
import jax.numpy as jnp
from jax import lax
from jax.experimental import pallas as pl
from jax.experimental.pallas import tpu as pltpu

F32 = jnp.float32
BF16 = jnp.bfloat16

HEAD_DIM = 128
GDN_HEADS = 8
GDN_CONV = 4
SWA_HEADS = 8
SWA_KV_HEADS = 2
SWA_GROUP = SWA_HEADS // SWA_KV_HEADS
WINDOW = 128
FFN_CONV = 3
N_MOD = 6
EPS = 1e-6
CHUNK = 64
PAST_LEN = 4096

GDN_WIDTH = GDN_HEADS * HEAD_DIM
SWA_WIDTH = SWA_HEADS * HEAD_DIM
KV_WIDTH = SWA_KV_HEADS * HEAD_DIM
QKV_D_WIDTH = 3 * GDN_WIDTH

LANES = 128
SUBLANES = 8
GDN_ROWS = 128
INV_BASE = 16
VMEM_LIMIT = 56 * 1024 * 1024

COL_QKV = 0
COL_Z = QKV_D_WIDTH
COL_QA = COL_Z + GDN_WIDTH
COL_KA = COL_QA + SWA_WIDTH
COL_VA = COL_KA + KV_WIDTH
COL_AB = COL_VA + KV_WIDTH
PROJ_WIDTH = COL_AB + LANES


def _dot(a, b):
    return jnp.dot(a.astype(BF16), b.astype(BF16), preferred_element_type=F32)


def _dot_nt(a, b):
    return lax.dot_general(a.astype(BF16), b.astype(BF16), (((1,), (1,)), ((), ())),
                           preferred_element_type=F32)


def _split_bf16(x):
    hi = x.astype(BF16)
    lo = (x - hi.astype(F32)).astype(BF16)
    return hi, lo


def _dot3(a, b):
    ah, al = _split_bf16(a)
    bh, bl = _split_bf16(b)
    d = functools.partial(jnp.dot, preferred_element_type=F32)
    return d(ah, bh) + (d(ah, bl) + d(al, bh))


def _silu(x):
    return x * jax.nn.sigmoid(x)


def _softplus(x):
    return jnp.maximum(x, 0.0) + jnp.log1p(jnp.exp(-jnp.abs(x)))


def _params(*sem):
    return pltpu.CompilerParams(dimension_semantics=sem, vmem_limit_bytes=VMEM_LIMIT)


def _mod_kernel(c_ref, w_ref, b_ref, o_ref):
    o_ref[...] = _dot(_silu(c_ref[...]), w_ref[...]) + b_ref[...]


def _modulation(c, ada_w, ada_b, tn=1024):
    rows, d = c.shape
    n = ada_w.shape[1]
    return pl.pallas_call(
        _mod_kernel,
        grid=(n // tn,),
        in_specs=[pl.BlockSpec((rows, d), lambda j: (0, 0)),
                  pl.BlockSpec((d, tn), lambda j: (0, j)),
                  pl.BlockSpec((1, tn), lambda j: (0, j))],
        out_specs=pl.BlockSpec((rows, tn), lambda j: (0, j)),
        out_shape=jax.ShapeDtypeStruct((rows, n), F32),
        compiler_params=_params("arbitrary"),
        name="modulation",
    )(c, ada_w, ada_b.reshape(1, n))


def _modulated_norm(x, nw, shift, scale):
    y = x * lax.rsqrt(jnp.mean(x * x, axis=-1, keepdims=True) + EPS) * nw
    return y * (1.0 + scale) + shift


def _inproj_kernel(x_ref, mod_ref, nw_ref, w_ref, o_ref, h_ref):
    tb, tl, d = x_ref.shape

    @pl.when(pl.program_id(2) == 0)
    def _():
        h = _modulated_norm(x_ref[...], nw_ref[...], mod_ref[:, 0:1, :], mod_ref[:, 1:2, :])
        h_ref[...] = h.reshape(tb * tl, d).astype(BF16)

    o_ref[...] = jnp.dot(h_ref[...], w_ref[...], preferred_element_type=F32).reshape(o_ref.shape)


def _inproj(x, mod, norm_w, w, tb, tl, tn):
    b, l, d = x.shape
    n = w.shape[1]
    return pl.pallas_call(
        _inproj_kernel,
        grid=(b // tb, l // tl, n // tn),
        in_specs=[pl.BlockSpec((tb, tl, d), lambda i, t, j: (i, t, 0)),
                  pl.BlockSpec((tb, N_MOD, d), lambda i, t, j: (i, 0, 0)),
                  pl.BlockSpec((1, d), lambda i, t, j: (0, 0)),
                  pl.BlockSpec((d, tn), lambda i, t, j: (0, j))],
        out_specs=pl.BlockSpec((tb, tl, tn), lambda i, t, j: (i, t, j)),
        out_shape=jax.ShapeDtypeStruct((b, l, n), F32),
        scratch_shapes=[pltpu.VMEM((tb * tl, d), BF16)],
        compiler_params=_params("arbitrary", "arbitrary", "arbitrary"),
        name="in_projection",
    )(x, mod, norm_w.reshape(1, d), w)


def _unit_lower_inverse(a, same_block, eye):
    sizes = sorted(same_block)
    p = -jnp.where(same_block[sizes[0]], a, 0.0)
    x = eye + p
    k = 2
    while k < sizes[0]:
        p = _dot3(p, p)
        x = x + _dot3(x, p)
        k *= 2
    for s_prev, s in zip(sizes[:-1], sizes[1:]):
        a_off = jnp.where(jnp.logical_and(same_block[s], jnp.logical_not(same_block[s_prev])), a, 0.0)
        x = x - _dot3(_dot3(x, a_off), x)
    return x


def _gdn_kernel(qkv_ref, z_ref, ab_ref, cst_ref, s0_ref, cw_ref, hp_ref, nw_ref,
                o_ref, sout_ref, ext_ref, s_ref):
    nb, c, _ = qkv_ref.shape
    r = nb * c
    t = pl.program_id(1)

    @pl.when(t == 0)
    def _():
        ext_ref[:, 0:SUBLANES, :] = cst_ref[...]
        s_ref[...] = s0_ref[...]

    ext_ref[:, SUBLANES:SUBLANES + c, :] = qkv_ref[...]

    ab = ab_ref[...].reshape(r, LANES)
    hp = hp_ref[...]
    g = -jnp.exp(hp[0:1, :]) * _softplus(ab + hp[1:2, :])
    beta = jax.nn.sigmoid(ab)
    pos = jnp.bitwise_and(lax.broadcasted_iota(jnp.int32, (r, LANES), 0), c - 1)
    gc = g
    s = 1
    while s < c:
        gc = gc + jnp.where(pos >= s, pltpu.roll(gc, s, 0), 0.0)
        s *= 2
    if nb == 1:
        gcl = jnp.broadcast_to(gc[r - 1:r, :], (r, LANES))
    else:
        gcl = jnp.where(pos == c - 1, gc, 0.0)
        s = 1
        while s < c:
            gcl = gcl + jnp.where(pos < c - s, pltpu.roll(gcl, r - s, 0), 0.0)
            s *= 2
    eg = jnp.exp(gc)
    egl = jnp.exp(gcl - gc)
    gl_rows = [jnp.exp(gc[sgi * c + c - 1:sgi * c + c, :]) for sgi in range(nb)]
    gct = gc.T

    row = lax.broadcasted_iota(jnp.int32, (r, r), 0)
    col = lax.broadcasted_iota(jnp.int32, (r, r), 1)

    def same(sz):
        sh = int(math.log2(sz))
        return jnp.right_shift(row, sh) == jnp.right_shift(col, sh)

    sizes = []
    sz = min(INV_BASE, c)
    while sz <= c:
        sizes.append(sz)
        sz *= 2
    same_block = {sz: same(sz) for sz in sizes}
    seg = same_block[c]
    incl = jnp.logical_and(seg, row >= col)
    strict = jnp.logical_and(seg, row > col)
    eye = (row == col).astype(F32)
    rowseg = jnp.right_shift(lax.broadcasted_iota(jnp.int32, (r, HEAD_DIM), 0), int(math.log2(c)))

    cw = cw_ref[...]

    def conv_cols(c0):
        acc = cw[0:1, c0:c0 + HEAD_DIM] * ext_ref[:, SUBLANES - 3:SUBLANES - 3 + c, c0:c0 + HEAD_DIM]
        for i in range(1, GDN_CONV):
            lo = SUBLANES - 3 + i
            acc = acc + cw[i:i + 1, c0:c0 + HEAD_DIM] * ext_ref[:, lo:lo + c, c0:c0 + HEAD_DIM]
        return _silu(acc).reshape(r, HEAD_DIM)

    nw = nw_ref[...]
    for h in range(GDN_HEADS):
        q = conv_cols(h * HEAD_DIM)
        k = conv_cols(GDN_WIDTH + h * HEAD_DIM)
        v = conv_cols(2 * GDN_WIDTH + h * HEAD_DIM)
        qn = q * (lax.rsqrt(jnp.sum(q * q, axis=-1, keepdims=True) + EPS) * (HEAD_DIM ** -0.5))
        kn = k * lax.rsqrt(jnp.sum(k * k, axis=-1, keepdims=True) + EPS)
        bcol = beta[:, SUBLANES + h:SUBLANES + h + 1]
        gcol = gc[:, h:h + 1]
        egcol = eg[:, h:h + 1]
        eglcol = egl[:, h:h + 1]
        grow = gct[h:h + 1, :]
        dec = jnp.exp(jnp.where(incl, gcol - grow, -jnp.inf))
        kb = kn * bcol
        kn16 = kn.astype(BF16)
        a = jnp.where(strict, _dot_nt(kb, kn16) * dec, 0.0)
        tinv = _unit_lower_inverse(a, same_block, eye)
        rhs = jnp.concatenate([v * bcol, kb * egcol], axis=1)
        sol = _dot3(tinv, rhs)
        u = sol[:, :HEAD_DIM]
        w = sol[:, HEAD_DIM:]
        qk = _dot_nt(qn, kn16) * dec
        qg = qn * egcol
        kd = kn * eglcol

        ws_parts, qs_parts = [], []
        for sgi in range(nb):
            lo = sgi * c
            wq = jnp.concatenate([w[lo:lo + c], qg[lo:lo + c]], axis=0)
            res = _dot(wq, s_ref[sgi, h])
            ws_parts.append(res[:c])
            qs_parts.append(res[c:])
        ws = ws_parts[0] if nb == 1 else jnp.concatenate(ws_parts, axis=0)
        qs = qs_parts[0] if nb == 1 else jnp.concatenate(qs_parts, axis=0)
        v_new = u - ws
        v16 = v_new.astype(BF16)
        o = qs + _dot(qk, v16)
        for sgi in range(nb):
            kdm = kd if nb == 1 else jnp.where(rowseg == sgi, kd, 0.0)
            s_ref[sgi, h] = s_ref[sgi, h] * gl_rows[sgi][:, h:h + 1] + _dot(kdm.T, v16)

        on = o * lax.rsqrt(jnp.mean(o * o, axis=-1, keepdims=True) + EPS) * nw
        zh = z_ref[:, :, h * HEAD_DIM:(h + 1) * HEAD_DIM].reshape(r, HEAD_DIM)
        out = on * _silu(zh)
        o_ref[:, :, h * HEAD_DIM:(h + 1) * HEAD_DIM] = out.reshape(nb, c, HEAD_DIM).astype(o_ref.dtype)

    ext_ref[:, 0:SUBLANES, :] = ext_ref[:, c:c + SUBLANES, :]

    @pl.when(t == pl.num_programs(1) - 1)
    def _():
        sout_ref[...] = s_ref[...]


def _gdn(proj, conv_state8, s0, conv_w8, head_params, norm_w, nb, c):
    b, l, _ = proj.shape
    nt = l // c
    zb = COL_Z // GDN_WIDTH
    abb = COL_AB // LANES
    return pl.pallas_call(
        _gdn_kernel,
        grid=(b // nb, nt),
        in_specs=[pl.BlockSpec((nb, c, QKV_D_WIDTH), lambda i, t: (i, t, 0)),
                  pl.BlockSpec((nb, c, GDN_WIDTH), lambda i, t: (i, t, zb)),
                  pl.BlockSpec((nb, c, LANES), lambda i, t: (i, t, abb)),
                  pl.BlockSpec((nb, SUBLANES, QKV_D_WIDTH), lambda i, t: (i, 0, 0)),
                  pl.BlockSpec((nb, GDN_HEADS, HEAD_DIM, HEAD_DIM), lambda i, t: (i, 0, 0, 0)),
                  pl.BlockSpec((SUBLANES, QKV_D_WIDTH), lambda i, t: (0, 0)),
                  pl.BlockSpec((SUBLANES, LANES), lambda i, t: (0, 0)),
                  pl.BlockSpec((1, HEAD_DIM), lambda i, t: (0, 0))],
        out_specs=[pl.BlockSpec((nb, c, GDN_WIDTH), lambda i, t: (i, t, 0)),
                   pl.BlockSpec((nb, GDN_HEADS, HEAD_DIM, HEAD_DIM), lambda i, t: (i, 0, 0, 0))],
        out_shape=[jax.ShapeDtypeStruct((b, l, GDN_WIDTH), BF16),
                   jax.ShapeDtypeStruct((b, GDN_HEADS, HEAD_DIM, HEAD_DIM), F32)],
        scratch_shapes=[pltpu.VMEM((nb, SUBLANES + c, QKV_D_WIDTH), F32),
                        pltpu.VMEM((nb, GDN_HEADS, HEAD_DIM, HEAD_DIM), F32)],
        compiler_params=_params("arbitrary", "arbitrary"),
        name="gated_delta_rule",
    )(proj, proj, proj, conv_state8, s0, conv_w8, head_params, norm_w.reshape(1, HEAD_DIM))


def _head_rms(x, w):
    return x * lax.rsqrt(jnp.mean(x * x, axis=-1, keepdims=True) + EPS) * w


def _swa_kernel(sink_ref, q_ref, k_ref, v_ref, kp_ref, vp_ref, qw_ref, kw_ref, o_ref, kn_ref,
                *, chunk, pos0, norm_prev):
    tq = q_ref.shape[1]
    tk = max(tq, LANES)
    m = SWA_GROUP * tq
    t0 = pos0 + pl.program_id(1) * tq
    qw = qw_ref[...]
    kw = kw_ref[...]

    rown = lax.broadcasted_iota(jnp.int32, (m, 1), 0)
    grp = rown // tq
    qi = rown - grp * tq
    cs = (qi // chunk) * chunk
    kj = lax.broadcasted_iota(jnp.int32, (1, tk), 1)
    pj = lax.broadcasted_iota(jnp.int32, (1, WINDOW), 1)
    vis_own = jnp.logical_and(kj < cs + chunk, kj < tq)
    vis_prev = jnp.logical_and(pj - WINDOW >= cs - WINDOW, t0 - WINDOW + pj >= 0)
    dist_own = jnp.abs(qi - kj).astype(F32)
    dist_prev = (qi + WINDOW - pj).astype(F32)
    scale = HEAD_DIM ** -0.5

    for kvh in range(SWA_KV_HEADS):
        slope = jnp.zeros((m, 1), F32)
        sink = jnp.zeros((m, 1), F32)
        for gi in range(SWA_GROUP):
            hq = kvh * SWA_GROUP + gi
            slope = jnp.where(grp == gi, 2.0 ** (-8.0 * (hq + 1) / SWA_HEADS), slope)
            sink = jnp.where(grp == gi, sink_ref[hq], sink)
        qs = jnp.concatenate(
            [_head_rms(q_ref[0, :, (kvh * SWA_GROUP + gi) * HEAD_DIM:(kvh * SWA_GROUP + gi + 1) * HEAD_DIM], qw)
             for gi in range(SWA_GROUP)], axis=0)
        ksl = slice(kvh * HEAD_DIM, (kvh + 1) * HEAD_DIM)
        k_own = _head_rms(k_ref[0, :, ksl], kw)
        kn_ref[0, :, ksl] = k_own
        v_own = v_ref[0, :, ksl]
        if tk > tq:
            pad = jnp.zeros((tk - tq, HEAD_DIM), F32)
            k_own = jnp.concatenate([k_own, pad], axis=0)
            v_own = jnp.concatenate([v_own, pad], axis=0)
        k_prev = kp_ref[0, :, ksl]
        if norm_prev:
            k_prev = _head_rms(k_prev, kw)
        v_prev = vp_ref[0, :, ksl]

        s_own = jnp.where(vis_own, _dot_nt(qs, k_own) * scale - slope * dist_own, -jnp.inf)
        s_prev = jnp.where(vis_prev, _dot_nt(qs, k_prev) * scale - slope * dist_prev, -jnp.inf)
        mx = jnp.maximum(jnp.maximum(jnp.max(s_own, axis=-1, keepdims=True),
                                     jnp.max(s_prev, axis=-1, keepdims=True)), sink)
        p_own = jnp.exp(s_own - mx)
        p_prev = jnp.exp(s_prev - mx)
        den = (jnp.sum(p_own, axis=-1, keepdims=True) + jnp.sum(p_prev, axis=-1, keepdims=True)
               + jnp.exp(sink - mx))
        inv = 1.0 / den
        o = _dot(p_own * inv, v_own) + _dot(p_prev * inv, v_prev)
        for gi in range(SWA_GROUP):
            hq = kvh * SWA_GROUP + gi
            o_ref[0, :, hq * HEAD_DIM:(hq + 1) * HEAD_DIM] = o[gi * tq:(gi + 1) * tq].astype(o_ref.dtype)


def _swa(proj, k_prev_src, v_prev_src, prev_from_proj, q_norm_w, k_norm_w, sinks, tq, chunk, pos0):
    b, l, _ = proj.shape
    qb = COL_QA // SWA_WIDTH
    kb = COL_KA // KV_WIDTH
    vb = COL_VA // KV_WIDTH
    if prev_from_proj:
        assert tq == WINDOW
        kp_spec = pl.BlockSpec((1, WINDOW, KV_WIDTH), lambda i, t: (i, jnp.maximum(t - 1, 0), kb))
        vp_spec = pl.BlockSpec((1, WINDOW, KV_WIDTH), lambda i, t: (i, jnp.maximum(t - 1, 0), vb))
    else:
        assert l == tq
        kp_spec = pl.BlockSpec((1, WINDOW, KV_WIDTH), lambda i, t: (i, 0, 0))
        vp_spec = pl.BlockSpec((1, WINDOW, KV_WIDTH), lambda i, t: (i, 0, 0))
    kernel = functools.partial(_swa_kernel, chunk=chunk, pos0=pos0, norm_prev=prev_from_proj)
    return pl.pallas_call(
        kernel,
        grid=(b, l // tq),
        in_specs=[pl.BlockSpec(memory_space=pltpu.SMEM),
                  pl.BlockSpec((1, tq, SWA_WIDTH), lambda i, t: (i, t, qb)),
                  pl.BlockSpec((1, tq, KV_WIDTH), lambda i, t: (i, t, kb)),
                  pl.BlockSpec((1, tq, KV_WIDTH), lambda i, t: (i, t, vb)),
                  kp_spec, vp_spec,
                  pl.BlockSpec((1, HEAD_DIM), lambda i, t: (0, 0)),
                  pl.BlockSpec((1, HEAD_DIM), lambda i, t: (0, 0))],
        out_specs=[pl.BlockSpec((1, tq, SWA_WIDTH), lambda i, t: (i, t, 0)),
                   pl.BlockSpec((1, tq, KV_WIDTH), lambda i, t: (i, t, 0))],
        out_shape=[jax.ShapeDtypeStruct((b, l, SWA_WIDTH), BF16),
                   jax.ShapeDtypeStruct((b, l, KV_WIDTH), F32)],
        compiler_params=_params("arbitrary", "arbitrary"),
        name="sliding_window_attention",
    )(sinks, proj, proj, proj, k_prev_src, v_prev_src,
      q_norm_w.reshape(1, HEAD_DIM), k_norm_w.reshape(1, HEAD_DIM))


def _outproj_kernel(x_ref, od_ref, oa_ref, mod_ref, w_ref, o_ref):
    tb, tl, d = x_ref.shape
    od = od_ref[...].reshape(tb * tl, GDN_WIDTH)
    oa = oa_ref[...].reshape(tb * tl, SWA_WIDTH)
    acc = (jnp.dot(od, w_ref[0:GDN_WIDTH, :], preferred_element_type=F32)
           + jnp.dot(oa, w_ref[GDN_WIDTH:GDN_WIDTH + SWA_WIDTH, :], preferred_element_type=F32))
    o_ref[...] = x_ref[...] + mod_ref[:, 2:3, :] * acc.reshape(tb, tl, d)


def _outproj(x, o_d, o_a, mod, w_o, tb, tl):
    b, l, d = x.shape
    return pl.pallas_call(
        _outproj_kernel,
        grid=(b // tb, l // tl),
        in_specs=[pl.BlockSpec((tb, tl, d), lambda i, t: (i, t, 0)),
                  pl.BlockSpec((tb, tl, GDN_WIDTH), lambda i, t: (i, t, 0)),
                  pl.BlockSpec((tb, tl, SWA_WIDTH), lambda i, t: (i, t, 0)),
                  pl.BlockSpec((tb, N_MOD, d), lambda i, t: (i, 0, 0)),
                  pl.BlockSpec(w_o.shape, lambda i, t: (0, 0))],
        out_specs=pl.BlockSpec((tb, tl, d), lambda i, t: (i, t, 0)),
        out_shape=jax.ShapeDtypeStruct((b, l, d), F32),
        compiler_params=_params("arbitrary", "arbitrary"),
        name="out_projection",
    )(x, o_d, o_a, mod, w_o)


def _ffn_kernel(x_ref, mod_ref, nw_ref, wg_ref, wu_ref, cwg_ref, cwu_ref, bg_ref, bu_ref,
                stg_ref, stu_ref, wd_ref, y_ref, lastg_ref, lastu_ref,
                h_ref, extg_ref, extu_ref, carg_ref, caru_ref):
    tb, tl, d = x_ref.shape
    tf = wg_ref.shape[1]
    t = pl.program_id(1)
    j = pl.program_id(2)

    @pl.when(j == 0)
    def _():
        h = _modulated_norm(x_ref[...], nw_ref[...], mod_ref[:, 3:4, :], mod_ref[:, 4:5, :])
        h_ref[...] = h.reshape(tb * tl, d).astype(BF16)

    h = h_ref[...]

    def conv_half(w_ref, cw_ref, b_ref, st_ref, ext_ref, car_ref, last_ref):
        u = jnp.dot(h, w_ref[...], preferred_element_type=F32)

        @pl.when(t == 0)
        def _():
            ext_ref[:, 0:SUBLANES, :] = st_ref[...]

        @pl.when(t > 0)
        def _():
            ext_ref[:, 0:SUBLANES, :] = car_ref[j]

        ext_ref[:, SUBLANES:SUBLANES + tl, :] = u.reshape(tb, tl, tf)
        tail = ext_ref[:, tl:tl + SUBLANES, :]
        car_ref[j] = tail
        last_ref[:, 0] = tail
        cw = cw_ref[...]
        acc = b_ref[...] + cw[0:1, :] * ext_ref[:, SUBLANES - 2:SUBLANES - 2 + tl, :]
        for i in range(1, FFN_CONV):
            lo = SUBLANES - 2 + i
            acc = acc + cw[i:i + 1, :] * ext_ref[:, lo:lo + tl, :]
        return acc

    gt = conv_half(wg_ref, cwg_ref, bg_ref, stg_ref, extg_ref, carg_ref, lastg_ref)
    up = conv_half(wu_ref, cwu_ref, bu_ref, stu_ref, extu_ref, caru_ref, lastu_ref)
    act = (_silu(gt) * up).reshape(tb * tl, tf).astype(BF16)
    contrib = jnp.dot(act, wd_ref[...], preferred_element_type=F32).reshape(tb, tl, d)

    @pl.when(j == 0)
    def _():
        y_ref[...] = contrib

    @pl.when(j > 0)
    def _():
        y_ref[...] += contrib

    @pl.when(j == pl.num_programs(2) - 1)
    def _():
        y_ref[...] = x_ref[...] + mod_ref[:, 5:6, :] * y_ref[...]


def _ffn(x, mod, norm_w, w_up, conv_w8, conv_b, state8, w_down, tb, tl, tf):
    b, l, d = x.shape
    d_ff = w_down.shape[0]
    nj = d_ff // tf
    cb = conv_b.reshape(1, 2 * d_ff)
    lo = lambda i, t, j: (0, j)
    hi = lambda i, t, j: (0, nj + j)
    return pl.pallas_call(
        _ffn_kernel,
        grid=(b // tb, l // tl, nj),
        in_specs=[pl.BlockSpec((tb, tl, d), lambda i, t, j: (i, t, 0)),
                  pl.BlockSpec((tb, N_MOD, d), lambda i, t, j: (i, 0, 0)),
                  pl.BlockSpec((1, d), lambda i, t, j: (0, 0)),
                  pl.BlockSpec((d, tf), lo), pl.BlockSpec((d, tf), hi),
                  pl.BlockSpec((SUBLANES, tf), lo), pl.BlockSpec((SUBLANES, tf), hi),
                  pl.BlockSpec((1, tf), lo), pl.BlockSpec((1, tf), hi),
                  pl.BlockSpec((tb, SUBLANES, tf), lambda i, t, j: (i, 0, j)),
                  pl.BlockSpec((tb, SUBLANES, tf), lambda i, t, j: (i, 0, nj + j)),
                  pl.BlockSpec((tf, d), lambda i, t, j: (j, 0))],
        out_specs=[pl.BlockSpec((tb, tl, d), lambda i, t, j: (i, t, 0)),
                   pl.BlockSpec((tb, 1, SUBLANES, tf), lambda i, t, j: (i, t, 0, j)),
                   pl.BlockSpec((tb, 1, SUBLANES, tf), lambda i, t, j: (i, t, 0, j))],
        out_shape=[jax.ShapeDtypeStruct((b, l, d), F32),
                   jax.ShapeDtypeStruct((b, l // tl, SUBLANES, d_ff), F32),
                   jax.ShapeDtypeStruct((b, l // tl, SUBLANES, d_ff), F32)],
        scratch_shapes=[pltpu.VMEM((tb * tl, d), BF16),
                        pltpu.VMEM((tb, SUBLANES + tl, tf), F32),
                        pltpu.VMEM((tb, SUBLANES + tl, tf), F32),
                        pltpu.VMEM((nj, tb, SUBLANES, tf), F32),
                        pltpu.VMEM((nj, tb, SUBLANES, tf), F32)],
        compiler_params=_params("arbitrary", "arbitrary", "arbitrary"),
        name="conv_ffn",
    )(x, mod, norm_w.reshape(1, d), w_up, w_up, conv_w8, conv_w8, cb, cb, state8, state8, w_down)


def _pad_rows_front(x, rows):
    pad = rows - x.shape[1]
    return jnp.pad(x, ((0, 0), (pad, 0), (0, 0)))


def _layer(x, mod, conv_prev, s0, k_prev, v_prev, ffn_prev, pos0, wts, cfg):
    b, l, d = x.shape
    proj = _inproj(x, mod, wts["norm1_w"], wts["w_in"], cfg["tb"], cfg["tl_in"], cfg["tn_in"])

    nb = cfg["gdn_nb"]
    c = GDN_ROWS // nb
    o_d, s_new = _gdn(proj, _pad_rows_front(conv_prev, SUBLANES), s0, wts["conv_qkv_w8"],
                      wts["head_params"], wts["gdn_norm_w"], nb, c)

    chunk = min(CHUNK, l)
    if k_prev is None:
        o_a, kn = _swa(proj, proj, proj, True, wts["q_norm_w"], wts["k_norm_w"], wts["sinks"],
                       cfg["tq"], chunk, pos0)
    else:
        o_a, kn = _swa(proj, k_prev.reshape(b, WINDOW, KV_WIDTH), v_prev.reshape(b, WINDOW, KV_WIDTH), False,
                       wts["q_norm_w"], wts["k_norm_w"], wts["sinks"], cfg["tq"], chunk, pos0)

    x1 = _outproj(x, o_d, o_a, mod, wts["w_o"], cfg["tb"], cfg["tl_out"])
    y, last_g, last_u = _ffn(x1, mod, wts["norm2_w"], wts["w_up"], wts["ffn_conv_w8"], wts["ffn_conv_b"],
                             _pad_rows_front(ffn_prev, SUBLANES), wts["w_down"],
                             cfg["tb"], cfg["tl_ffn"], cfg["tf"])

    qkv_new = proj[:, l - (GDN_CONV - 1):, COL_QKV:COL_QKV + QKV_D_WIDTH]
    if l >= GDN_CONV - 1:
        conv_new = qkv_new
    else:
        conv_new = jnp.concatenate([conv_prev, qkv_new], axis=1)[:, -(GDN_CONV - 1):]
    v_new = proj[:, :, COL_VA:COL_VA + KV_WIDTH]
    if l >= WINDOW:
        k_cache = kn[:, l - WINDOW:]
        v_cache = v_new[:, l - WINDOW:]
    else:
        k_cache = jnp.concatenate([k_prev.reshape(b, WINDOW, KV_WIDTH)[:, l:], kn], axis=1)
        v_cache = jnp.concatenate([v_prev.reshape(b, WINDOW, KV_WIDTH)[:, l:], v_new], axis=1)
    k_cache = k_cache.reshape(b, WINDOW, SWA_KV_HEADS, HEAD_DIM)
    v_cache = v_cache.reshape(b, WINDOW, SWA_KV_HEADS, HEAD_DIM)
    ffn_new = jnp.concatenate([last_g[:, -1, SUBLANES - (FFN_CONV - 1):],
                               last_u[:, -1, SUBLANES - (FFN_CONV - 1):]], axis=-1)
    return y, conv_new, s_new, k_cache, v_cache, ffn_new


def _prep_weights(l, ada_w, ada_b, norm1_w, norm2_w, w_in, conv_qkv_w, a_log, dt_bias, gdn_norm_w,
                  q_norm_w, k_norm_w, sinks, w_o, w_up, ffn_conv_w, ffn_conv_b, w_down):
    d = w_in.shape[1]
    ab0 = QKV_D_WIDTH + GDN_WIDTH
    ab1 = ab0 + 2 * GDN_HEADS
    w = w_in[l]
    w_in_r = jnp.concatenate(
        [w[:, :ab0], w[:, ab1:], w[:, ab0:ab1], jnp.zeros((d, LANES - 2 * GDN_HEADS), w.dtype)], axis=1).astype(BF16)
    assert w_in_r.shape[1] == PROJ_WIDTH
    hp = jnp.zeros((SUBLANES, LANES), F32)
    hp = hp.at[0, :GDN_HEADS].set(a_log[l].astype(F32)).at[1, :GDN_HEADS].set(dt_bias[l].astype(F32))
    return {
        "ada_w": ada_w[l], "ada_b": ada_b[l], "norm1_w": norm1_w[l], "norm2_w": norm2_w[l],
        "w_in": w_in_r,
        "conv_qkv_w8": jnp.pad(conv_qkv_w[l], ((0, SUBLANES - GDN_CONV), (0, 0))),
        "head_params": hp,
        "gdn_norm_w": gdn_norm_w[l], "q_norm_w": q_norm_w[l], "k_norm_w": k_norm_w[l], "sinks": sinks[l],
        "w_o": w_o[l].astype(BF16), "w_up": w_up[l].astype(BF16),
        "ffn_conv_w8": jnp.pad(ffn_conv_w[l], ((0, SUBLANES - FFN_CONV), (0, 0))),
        "ffn_conv_b": ffn_conv_b[l], "w_down": w_down[l].astype(BF16),
    }


PROMPT_CFG = dict(tb=1, tl_in=1024, tn_in=1152, gdn_nb=1, tq=128, tl_out=512, tl_ffn=512, tf=512)


def _sample_cfg(b, l):
    return dict(tb=b, tl_in=l, tn_in=1152, gdn_nb=GDN_ROWS // l, tq=l, tl_out=l, tl_ffn=l, tf=512)


def kernel(x_prompt, x_sample, state_conv_qkv, state_delta, cache_swa_k, cache_swa_v, state_ffn_conv,
           c_prompt, c_sample, ada_w, ada_b, norm1_w, norm2_w, w_in, conv_qkv_w, a_log, dt_bias,
           gdn_norm_w, q_norm_w, k_norm_w, sinks, w_o, w_up, ffn_conv_w, ffn_conv_b, w_down):
    depth = w_in.shape[0]
    bp, lp, d = x_prompt.shape
    bs, ls, _ = x_sample.shape
    d_ff = w_down.shape[1]
    xp, xs = x_prompt, x_sample
    acc_p = [[] for _ in range(5)]
    acc_s = [[] for _ in range(5)]
    c_all = jnp.concatenate([c_prompt, c_sample], axis=0)
    c_rows = -(-c_all.shape[0] // SUBLANES) * SUBLANES
    c_all = jnp.pad(c_all, ((0, c_rows - c_all.shape[0]), (0, 0)))
    for l in range(depth):
        wts = _prep_weights(l, ada_w, ada_b, norm1_w, norm2_w, w_in, conv_qkv_w, a_log, dt_bias, gdn_norm_w,
                            q_norm_w, k_norm_w, sinks, w_o, w_up, ffn_conv_w, ffn_conv_b, w_down)
        mod = _modulation(c_all, wts["ada_w"], wts["ada_b"]).reshape(c_rows, N_MOD, d)
        mod_p, mod_s = mod[:bp], mod[bp:bp + bs]
        dt = xp.dtype
        zero_state = (jnp.zeros((bp, GDN_CONV - 1, QKV_D_WIDTH), dt),
                      jnp.zeros((bp, GDN_HEADS, HEAD_DIM, HEAD_DIM), dt),
                      None, None,
                      jnp.zeros((bp, FFN_CONV - 1, 2 * d_ff), dt))
        xp, *new_p = _layer(xp, mod_p, *zero_state, 0, wts, PROMPT_CFG)
        xs, *new_s = _layer(xs, mod_s, state_conv_qkv[l], state_delta[l], cache_swa_k[l], cache_swa_v[l],
                            state_ffn_conv[l], PAST_LEN, wts, _sample_cfg(bs, ls))
        for acc, t in zip(acc_p, new_p):
            acc.append(t)
        for acc, t in zip(acc_s, new_s):
            acc.append(t)
    outs_p = [jnp.stack(a) for a in acc_p]
    outs_s = [jnp.stack(a) for a in acc_s]
    return (xp, xs, *outs_p, *outs_s)
```

```python
import functools
import math

import jax
import jax.numpy as jnp
from jax import lax
from jax.experimental import pallas as pl
from jax.experimental.pallas import tpu as pltpu

F32 = jnp.float32
BF16 = jnp.bfloat16

HEAD_DIM = 128
GDN_HEADS = 8
GDN_CONV = 4
SWA_HEADS = 8
SWA_KV_HEADS = 2
SWA_GROUP = SWA_HEADS // SWA_KV_HEADS
WINDOW = 128
FFN_CONV = 3
N_MOD = 6
EPS = 1e-6
CHUNK = 64
PAST_LEN = 4096

GDN_WIDTH = GDN_HEADS * HEAD_DIM
SWA_WIDTH = SWA_HEADS * HEAD_DIM
KV_WIDTH = SWA_KV_HEADS * HEAD_DIM
QKV_D_WIDTH = 3 * GDN_WIDTH

LANES = 128
SUBLANES = 8
GDN_ROWS = 128
INV_BASE = 16
VMEM_LIMIT = 56 * 1024 * 1024

COL_QKV = 0
COL_Z = QKV_D_WIDTH
COL_QA = COL_Z + GDN_WIDTH
COL_KA = COL_QA + SWA_WIDTH
COL_VA = COL_KA + KV_WIDTH
COL_AB = COL_VA + KV_WIDTH
PROJ_WIDTH = COL_AB + LANES


def _dot(a, b):
    return jnp.dot(a.astype(BF16), b.astype(BF16), preferred_element_type=F32)


def _dot_nt(a, b):
    return lax.dot_general(a.astype(BF16), b.astype(BF16), (((1,), (1,)), ((), ())),
                           preferred_element_type=F32)


def _silu(x):
    return x * jax.nn.sigmoid(x)


def _softplus(x):
    return jnp.maximum(x, 0.0) + jnp.log1p(jnp.exp(-jnp.abs(x)))


def _params(*sem):
    return pltpu.CompilerParams(dimension_semantics=sem, vmem_limit_bytes=VMEM_LIMIT)


def _mod_kernel(c_ref, w_ref, b_ref, o_ref):
    o_ref[...] = _dot(_silu(c_ref[...]), w_ref[...]) + b_ref[...]


def _modulation(c, ada_w, ada_b, tn=1024):
    rows, d = c.shape
    n = ada_w.shape[1]
    return pl.pallas_call(
        _mod_kernel,
        grid=(n // tn,),
        in_specs=[pl.BlockSpec((rows, d), lambda j: (0, 0)),
                  pl.BlockSpec((d, tn), lambda j: (0, j)),
                  pl.BlockSpec((1, tn), lambda j: (0, j))],
        out_specs=pl.BlockSpec((rows, tn), lambda j: (0, j)),
        out_shape=jax.ShapeDtypeStruct((rows, n), F32),
        compiler_params=_params("arbitrary"),
        name="modulation",
    )(c, ada_w, ada_b.reshape(1, n))


def _modulated_norm(x, nw, shift, scale):
    y = x * lax.rsqrt(jnp.mean(x * x, axis=-1, keepdims=True) + EPS) * nw
    return y * (1.0 + scale) + shift


def _inproj_kernel(x_ref, mod_ref, nw_ref, w_ref, o_ref, h_ref):
    tb, tl, d = x_ref.shape

    @pl.when(pl.program_id(2) == 0)
    def _():
        h = _modulated_norm(x_ref[...], nw_ref[...], mod_ref[:, 0:1, :], mod_ref[:, 1:2, :])
        h_ref[...] = h.reshape(tb * tl, d).astype(BF16)

    o_ref[...] = jnp.dot(h_ref[...], w_ref[...], preferred_element_type=F32).reshape(o_ref.shape)


def _inproj(x, mod, norm_w, w, tb, tl, tn):
    b, l, d = x.shape
    n = w.shape[1]
    return pl.pallas_call(
        _inproj_kernel,
        grid=(b // tb, l // tl, n // tn),
        in_specs=[pl.BlockSpec((tb, tl, d), lambda i, t, j: (i, t, 0)),
                  pl.BlockSpec((tb, N_MOD, d), lambda i, t, j: (i, 0, 0)),
                  pl.BlockSpec((1, d), lambda i, t, j: (0, 0)),
                  pl.BlockSpec((d, tn), lambda i, t, j: (0, j))],
        out_specs=pl.BlockSpec((tb, tl, tn), lambda i, t, j: (i, t, j)),
        out_shape=jax.ShapeDtypeStruct((b, l, n), F32),
        scratch_shapes=[pltpu.VMEM((tb * tl, d), BF16)],
        compiler_params=_params("arbitrary", "arbitrary", "arbitrary"),
        name="in_projection",
    )(x, mod, norm_w.reshape(1, d), w)


def _unit_lower_inverse(a_list, same_block, eye):
    sizes = sorted(same_block)
    p = [-jnp.where(same_block[sizes[0]], a, 0.0) for a in a_list]
    x = [eye + pi for pi in p]
    k = 2
    while k < sizes[0]:
        p = [_dot(pi, pi) for pi in p]
        x = [xi + _dot(xi, pi) for xi, pi in zip(x, p)]
        k *= 2
    for s_prev, s in zip(sizes[:-1], sizes[1:]):
        off = jnp.logical_and(same_block[s], jnp.logical_not(same_block[s_prev]))
        xa = [_dot(xi, jnp.where(off, a, 0.0)) for xi, a in zip(x, a_list)]
        x = [xi - _dot(xai, xi) for xi, xai in zip(x, xa)]
    return x


def _gdn_kernel(qkv_ref, z_ref, ab_ref, cst_ref, s0_ref, cw_ref, hp_ref, nw_ref,
                o_ref, sout_ref, ext_ref, s_ref):
    nb, c, _ = qkv_ref.shape
    r = nb * c
    t = pl.program_id(1)

    @pl.when(t == 0)
    def _():
        ext_ref[:, 0:SUBLANES, :] = cst_ref[...]
        s_ref[...] = s0_ref[...]

    ext_ref[:, SUBLANES:SUBLANES + c, :] = qkv_ref[...]

    ab = ab_ref[...].reshape(r, LANES)
    hp = hp_ref[...]
    g = -jnp.exp(hp[0:1, :]) * _softplus(ab + hp[1:2, :])
    beta = jax.nn.sigmoid(ab)
    pos = jnp.bitwise_and(lax.broadcasted_iota(jnp.int32, (r, LANES), 0), c - 1)
    gc = g
    s = 1
    while s < c:
        gc = gc + jnp.where(pos >= s, pltpu.roll(gc, s, 0), 0.0)
        s *= 2
    if nb == 1:
        gcl = jnp.broadcast_to(gc[r - 1:r, :], (r, LANES))
    else:
        gcl = jnp.where(pos == c - 1, gc, 0.0)
        s = 1
        while s < c:
            gcl = gcl + jnp.where(pos < c - s, pltpu.roll(gcl, r - s, 0), 0.0)
            s *= 2
    eg = jnp.exp(gc)
    egl = jnp.exp(gcl - gc)
    gl_rows = [jnp.exp(gc[sgi * c + c - 1:sgi * c + c, :]) for sgi in range(nb)]
    gct = gc.T

    row = lax.broadcasted_iota(jnp.int32, (r, r), 0)
    col = lax.broadcasted_iota(jnp.int32, (r, r), 1)

    def same(sz):
        sh = int(math.log2(sz))
        return jnp.right_shift(row, sh) == jnp.right_shift(col, sh)

    sizes = []
    sz = min(INV_BASE, c)
    while sz <= c:
        sizes.append(sz)
        sz *= 2
    same_block = {sz: same(sz) for sz in sizes}
    seg = same_block[c]
    incl = jnp.logical_and(seg, row >= col)
    strict = jnp.logical_and(seg, row > col)
    eye = (row == col).astype(F32)
    colseg = jnp.right_shift(lax.broadcasted_iota(jnp.int32, (HEAD_DIM, r), 1), int(math.log2(c)))

    cw = cw_ref[...]

    def conv_cols(c0):
        acc = cw[0:1, c0:c0 + HEAD_DIM] * ext_ref[:, SUBLANES - 3:SUBLANES - 3 + c, c0:c0 + HEAD_DIM]
        for i in range(1, GDN_CONV):
            lo = SUBLANES - 3 + i
            acc = acc + cw[i:i + 1, c0:c0 + HEAD_DIM] * ext_ref[:, lo:lo + c, c0:c0 + HEAD_DIM]
        return _silu(acc).reshape(r, HEAD_DIM)

    nw = nw_ref[...]
    heads = range(GDN_HEADS)
    q = [conv_cols(h * HEAD_DIM) for h in heads]
    k = [conv_cols(GDN_WIDTH + h * HEAD_DIM) for h in heads]
    v = [conv_cols(2 * GDN_WIDTH + h * HEAD_DIM) for h in heads]
    qn = [x * (lax.rsqrt(jnp.sum(x * x, axis=-1, keepdims=True) + EPS) * (HEAD_DIM ** -0.5)) for x in q]
    kn = [x * lax.rsqrt(jnp.sum(x * x, axis=-1, keepdims=True) + EPS) for x in k]
    bcol = [beta[:, SUBLANES + h:SUBLANES + h + 1] for h in heads]
    egcol = [eg[:, h:h + 1] for h in heads]
    dec = [jnp.exp(jnp.where(incl, gc[:, h:h + 1] - gct[h:h + 1, :], -jnp.inf)) for h in heads]
    kb = [kn[h] * bcol[h] for h in heads]
    kn16 = [x.astype(BF16) for x in kn]
    kq = [_dot_nt(jnp.concatenate([kb[h].astype(BF16), qn[h].astype(BF16)], axis=0), kn16[h]) for h in heads]
    a = [jnp.where(strict, kq[h][:r] * dec[h], 0.0) for h in heads]
    qk16 = [(kq[h][r:] * dec[h]).astype(BF16) for h in heads]
    tinv = _unit_lower_inverse(a, same_block, eye)
    sol = [_dot(tinv[h], jnp.concatenate([v[h] * bcol[h], kb[h] * egcol[h]], axis=1)) for h in heads]
    qg = [qn[h] * egcol[h] for h in heads]
    kdt16 = [(kn[h] * egl[:, h:h + 1]).T.astype(BF16) for h in heads]

    ws, qs = [], []
    for h in heads:
        w = sol[h][:, HEAD_DIM:]
        ws_parts, qs_parts = [], []
        for sgi in range(nb):
            lo = sgi * c
            wq = jnp.concatenate([w[lo:lo + c], qg[h][lo:lo + c]], axis=0)
            res = _dot(wq, s_ref[sgi, h])
            ws_parts.append(res[:c])
            qs_parts.append(res[c:])
        ws.append(ws_parts[0] if nb == 1 else jnp.concatenate(ws_parts, axis=0))
        qs.append(qs_parts[0] if nb == 1 else jnp.concatenate(qs_parts, axis=0))
    v16 = [(sol[h][:, :HEAD_DIM] - ws[h]).astype(BF16) for h in heads]
    o = [qs[h] + jnp.dot(qk16[h], v16[h], preferred_element_type=F32) for h in heads]
    for h in heads:
        for sgi in range(nb):
            if nb == 1:
                kdm = kdt16[h]
            else:
                kdm = jnp.where(colseg == sgi, kdt16[h], jnp.zeros_like(kdt16[h]))
            s_ref[sgi, h] = (s_ref[sgi, h] * gl_rows[sgi][:, h:h + 1]
                             + jnp.dot(kdm, v16[h], preferred_element_type=F32))
    for h in heads:
        on = o[h] * lax.rsqrt(jnp.mean(o[h] * o[h], axis=-1, keepdims=True) + EPS) * nw
        zh = z_ref[:, :, h * HEAD_DIM:(h + 1) * HEAD_DIM].reshape(r, HEAD_DIM)
        out = on * _silu(zh)
        o_ref[:, :, h * HEAD_DIM:(h + 1) * HEAD_DIM] = out.reshape(nb, c, HEAD_DIM).astype(o_ref.dtype)

    ext_ref[:, 0:SUBLANES, :] = ext_ref[:, c:c + SUBLANES, :]

    @pl.when(t == pl.num_programs(1) - 1)
    def _():
        sout_ref[...] = s_ref[...]


def _gdn(proj, conv_state8, s0, conv_w8, head_params, norm_w, nb, c):
    b, l, _ = proj.shape
    nt = l // c
    zb = COL_Z // GDN_WIDTH
    abb = COL_AB // LANES
    return pl.pallas_call(
        _gdn_kernel,
        grid=(b // nb, nt),
        in_specs=[pl.BlockSpec((nb, c, QKV_D_WIDTH), lambda i, t: (i, t, 0)),
                  pl.BlockSpec((nb, c, GDN_WIDTH), lambda i, t: (i, t, zb)),
                  pl.BlockSpec((nb, c, LANES), lambda i, t: (i, t, abb)),
                  pl.BlockSpec((nb, SUBLANES, QKV_D_WIDTH), lambda i, t: (i, 0, 0)),
                  pl.BlockSpec((nb, GDN_HEADS, HEAD_DIM, HEAD_DIM), lambda i, t: (i, 0, 0, 0)),
                  pl.BlockSpec((SUBLANES, QKV_D_WIDTH), lambda i, t: (0, 0)),
                  pl.BlockSpec((SUBLANES, LANES), lambda i, t: (0, 0)),
                  pl.BlockSpec((1, HEAD_DIM), lambda i, t: (0, 0))],
        out_specs=[pl.BlockSpec((nb, c, GDN_WIDTH), lambda i, t: (i, t, 0)),
                   pl.BlockSpec((nb, GDN_HEADS, HEAD_DIM, HEAD_DIM), lambda i, t: (i, 0, 0, 0))],
        out_shape=[jax.ShapeDtypeStruct((b, l, GDN_WIDTH), BF16),
                   jax.ShapeDtypeStruct((b, GDN_HEADS, HEAD_DIM, HEAD_DIM), F32)],
        scratch_shapes=[pltpu.VMEM((nb, SUBLANES + c, QKV_D_WIDTH), F32),
                        pltpu.VMEM((nb, GDN_HEADS, HEAD_DIM, HEAD_DIM), F32)],
        compiler_params=_params("arbitrary", "arbitrary"),
        name="gated_delta_rule",
    )(proj, proj, proj, conv_state8, s0, conv_w8, head_params, norm_w.reshape(1, HEAD_DIM))


def _head_rms(x, w):
    return x * lax.rsqrt(jnp.mean(x * x, axis=-1, keepdims=True) + EPS) * w


def _swa_kernel(sink_ref, q_ref, k_ref, v_ref, kp_ref, vp_ref, qw_ref, kw_ref, o_ref, kn_ref,
                *, chunk, pos0, norm_prev):
    tq = q_ref.shape[1]
    tk = max(tq, LANES)
    m = SWA_GROUP * tq
    t0 = pos0 + pl.program_id(1) * tq
    qw = qw_ref[...]
    kw = kw_ref[...]

    rown = lax.broadcasted_iota(jnp.int32, (m, 1), 0)
    grp = rown // tq
    qi = rown - grp * tq
    cs = (qi // chunk) * chunk
    kj = lax.broadcasted_iota(jnp.int32, (1, tk), 1)
    pj = lax.broadcasted_iota(jnp.int32, (1, WINDOW), 1)
    vis_own = jnp.logical_and(kj < cs + chunk, kj < tq)
    vis_prev = jnp.logical_and(pj - WINDOW >= cs - WINDOW, t0 - WINDOW + pj >= 0)
    dist_own = jnp.abs(qi - kj).astype(F32)
    dist_prev = (qi + WINDOW - pj).astype(F32)
    scale = HEAD_DIM ** -0.5

    for kvh in range(SWA_KV_HEADS):
        slope = jnp.zeros((m, 1), F32)
        sink = jnp.zeros((m, 1), F32)
        for gi in range(SWA_GROUP):
            hq = kvh * SWA_GROUP + gi
            slope = jnp.where(grp == gi, 2.0 ** (-8.0 * (hq + 1) / SWA_HEADS), slope)
            sink = jnp.where(grp == gi, sink_ref[hq], sink)
        qs = jnp.concatenate(
            [_head_rms(q_ref[0, :, (kvh * SWA_GROUP + gi) * HEAD_DIM:(kvh * SWA_GROUP + gi + 1) * HEAD_DIM], qw)
             for gi in range(SWA_GROUP)], axis=0)
        ksl = slice(kvh * HEAD_DIM, (kvh + 1) * HEAD_DIM)
        k_own = _head_rms(k_ref[0, :, ksl], kw)
        kn_ref[0, :, ksl] = k_own
        v_own = v_ref[0, :, ksl]
        if tk > tq:
            pad = jnp.zeros((tk - tq, HEAD_DIM), F32)
            k_own = jnp.concatenate([k_own, pad], axis=0)
            v_own = jnp.concatenate([v_own, pad], axis=0)
        k_prev = kp_ref[0, :, ksl]
        if norm_prev:
            k_prev = _head_rms(k_prev, kw)
        v_prev = vp_ref[0, :, ksl]

        s_own = jnp.where(vis_own, _dot_nt(qs, k_own) * scale - slope * dist_own, -jnp.inf)
        s_prev = jnp.where(vis_prev, _dot_nt(qs, k_prev) * scale - slope * dist_prev, -jnp.inf)
        mx = jnp.maximum(jnp.maximum(jnp.max(s_own, axis=-1, keepdims=True),
                                     jnp.max(s_prev, axis=-1, keepdims=True)), sink)
        p_own = jnp.exp(s_own - mx)
        p_prev = jnp.exp(s_prev - mx)
        den = (jnp.sum(p_own, axis=-1, keepdims=True) + jnp.sum(p_prev, axis=-1, keepdims=True)
               + jnp.exp(sink - mx))
        inv = 1.0 / den
        o = _dot(p_own * inv, v_own) + _dot(p_prev * inv, v_prev)
        for gi in range(SWA_GROUP):
            hq = kvh * SWA_GROUP + gi
            o_ref[0, :, hq * HEAD_DIM:(hq + 1) * HEAD_DIM] = o[gi * tq:(gi + 1) * tq].astype(o_ref.dtype)


def _swa(proj, k_prev_src, v_prev_src, prev_from_proj, q_norm_w, k_norm_w, sinks, tq, chunk, pos0):
    b, l, _ = proj.shape
    qb = COL_QA // SWA_WIDTH
    kb = COL_KA // KV_WIDTH
    vb = COL_VA // KV_WIDTH
    if prev_from_proj:
        assert tq == WINDOW
        kp_spec = pl.BlockSpec((1, WINDOW, KV_WIDTH), lambda i, t: (i, jnp.maximum(t - 1, 0), kb))
        vp_spec = pl.BlockSpec((1, WINDOW, KV_WIDTH), lambda i, t: (i, jnp.maximum(t - 1, 0), vb))
    else:
        assert l == tq
        kp_spec = pl.BlockSpec((1, WINDOW, KV_WIDTH), lambda i, t: (i, 0, 0))
        vp_spec = pl.BlockSpec((1, WINDOW, KV_WIDTH), lambda i, t: (i, 0, 0))
    kernel = functools.partial(_swa_kernel, chunk=chunk, pos0=pos0, norm_prev=prev_from_proj)
    return pl.pallas_call(
        kernel,
        grid=(b, l // tq),
        in_specs=[pl.BlockSpec(memory_space=pltpu.SMEM),
                  pl.BlockSpec((1, tq, SWA_WIDTH), lambda i, t: (i, t, qb)),
                  pl.BlockSpec((1, tq, KV_WIDTH), lambda i, t: (i, t, kb)),
                  pl.BlockSpec((1, tq, KV_WIDTH), lambda i, t: (i, t, vb)),
                  kp_spec, vp_spec,
                  pl.BlockSpec((1, HEAD_DIM), lambda i, t: (0, 0)),
                  pl.BlockSpec((1, HEAD_DIM), lambda i, t: (0, 0))],
        out_specs=[pl.BlockSpec((1, tq, SWA_WIDTH), lambda i, t: (i, t, 0)),
                   pl.BlockSpec((1, tq, KV_WIDTH), lambda i, t: (i, t, 0))],
        out_shape=[jax.ShapeDtypeStruct((b, l, SWA_WIDTH), BF16),
                   jax.ShapeDtypeStruct((b, l, KV_WIDTH), F32)],
        compiler_params=_params("arbitrary", "arbitrary"),
        name="sliding_window_attention",
    )(sinks, proj, proj, proj, k_prev_src, v_prev_src,
      q_norm_w.reshape(1, HEAD_DIM), k_norm_w.reshape(1, HEAD_DIM))


def _outproj_kernel(x_ref, od_ref, oa_ref, mod_ref, w_ref, o_ref):
    tb, tl, d = x_ref.shape
    od = od_ref[...].reshape(tb * tl, GDN_WIDTH)
    oa = oa_ref[...].reshape(tb * tl, SWA_WIDTH)
    acc = (jnp.dot(od, w_ref[0:GDN_WIDTH, :], preferred_element_type=F32)
           + jnp.dot(oa, w_ref[GDN_WIDTH:GDN_WIDTH + SWA_WIDTH, :], preferred_element_type=F32))
    o_ref[...] = x_ref[...] + mod_ref[:, 2:3, :] * acc.reshape(tb, tl, d)


def _outproj(x, o_d, o_a, mod, w_o, tb, tl):
    b, l, d = x.shape
    return pl.pallas_call(
        _outproj_kernel,
        grid=(b // tb, l // tl),
        in_specs=[pl.BlockSpec((tb, tl, d), lambda i, t: (i, t, 0)),
                  pl.BlockSpec((tb, tl, GDN_WIDTH), lambda i, t: (i, t, 0)),
                  pl.BlockSpec((tb, tl, SWA_WIDTH), lambda i, t: (i, t, 0)),
                  pl.BlockSpec((tb, N_MOD, d), lambda i, t: (i, 0, 0)),
                  pl.BlockSpec(w_o.shape, lambda i, t: (0, 0))],
        out_specs=pl.BlockSpec((tb, tl, d), lambda i, t: (i, t, 0)),
        out_shape=jax.ShapeDtypeStruct((b, l, d), F32),
        compiler_params=_params("arbitrary", "arbitrary"),
        name="out_projection",
    )(x, o_d, o_a, mod, w_o)


def _ffn_kernel(x_ref, mod_ref, nw_ref, wg_ref, wu_ref, cwg_ref, cwu_ref, bg_ref, bu_ref,
                stg_ref, stu_ref, wd_ref, y_ref, lastg_ref, lastu_ref,
                h_ref, extg_ref, extu_ref, carg_ref, caru_ref):
    tb, tl, d = x_ref.shape
    tf = wg_ref.shape[1]
    t = pl.program_id(1)
    j = pl.program_id(2)

    @pl.when(j == 0)
    def _():
        h = _modulated_norm(x_ref[...], nw_ref[...], mod_ref[:, 3:4, :], mod_ref[:, 4:5, :])
        h_ref[...] = h.reshape(tb * tl, d).astype(BF16)

    h = h_ref[...]

    def conv_half(w_ref, cw_ref, b_ref, st_ref, ext_ref, car_ref, last_ref):
        u = jnp.dot(h, w_ref[...], preferred_element_type=F32)

        @pl.when(t == 0)
        def _():
            ext_ref[:, 0:SUBLANES, :] = st_ref[...]

        @pl.when(t > 0)
        def _():
            ext_ref[:, 0:SUBLANES, :] = car_ref[j]

        ext_ref[:, SUBLANES:SUBLANES + tl, :] = u.reshape(tb, tl, tf)
        tail = ext_ref[:, tl:tl + SUBLANES, :]
        car_ref[j] = tail
        last_ref[:, 0] = tail
        cw = cw_ref[...]
        acc = b_ref[...] + cw[0:1, :] * ext_ref[:, SUBLANES - 2:SUBLANES - 2 + tl, :]
        for i in range(1, FFN_CONV):
            lo = SUBLANES - 2 + i
            acc = acc + cw[i:i + 1, :] * ext_ref[:, lo:lo + tl, :]
        return acc

    gt = conv_half(wg_ref, cwg_ref, bg_ref, stg_ref, extg_ref, carg_ref, lastg_ref)
    up = conv_half(wu_ref, cwu_ref, bu_ref, stu_ref, extu_ref, caru_ref, lastu_ref)
    act = (_silu(gt) * up).reshape(tb * tl, tf).astype(BF16)
    contrib = jnp.dot(act, wd_ref[...], preferred_element_type=F32).reshape(tb, tl, d)

    @pl.when(j == 0)
    def _():
        y_ref[...] = contrib

    @pl.when(j > 0)
    def _():
        y_ref[...] += contrib

    @pl.when(j == pl.num_programs(2) - 1)
    def _():
        y_ref[...] = x_ref[...] + mod_ref[:, 5:6, :] * y_ref[...]


def _ffn(x, mod, norm_w, w_up, conv_w8, conv_b, state8, w_down, tb, tl, tf):
    b, l, d = x.shape
    d_ff = w_down.shape[0]
    nj = d_ff // tf
    cb = conv_b.reshape(1, 2 * d_ff)
    lo = lambda i, t, j: (0, j)
    hi = lambda i, t, j: (0, nj + j)
    return pl.pallas_call(
        _ffn_kernel,
        grid=(b // tb, l // tl, nj),
        in_specs=[pl.BlockSpec((tb, tl, d), lambda i, t, j: (i, t, 0)),
                  pl.BlockSpec((tb, N_MOD, d), lambda i, t, j: (i, 0, 0)),
                  pl.BlockSpec((1, d), lambda i, t, j: (0, 0)),
                  pl.BlockSpec((d, tf), lo), pl.BlockSpec((d, tf), hi),
                  pl.BlockSpec((SUBLANES, tf), lo), pl.BlockSpec((SUBLANES, tf), hi),
                  pl.BlockSpec((1, tf), lo), pl.BlockSpec((1, tf), hi),
                  pl.BlockSpec((tb, SUBLANES, tf), lambda i, t, j: (i, 0, j)),
                  pl.BlockSpec((tb, SUBLANES, tf), lambda i, t, j: (i, 0, nj + j)),
                  pl.BlockSpec((tf, d), lambda i, t, j: (j, 0))],
        out_specs=[pl.BlockSpec((tb, tl, d), lambda i, t, j: (i, t, 0)),
                   pl.BlockSpec((tb, 1, SUBLANES, tf), lambda i, t, j: (i, t, 0, j)),
                   pl.BlockSpec((tb, 1, SUBLANES, tf), lambda i, t, j: (i, t, 0, j))],
        out_shape=[jax.ShapeDtypeStruct((b, l, d), F32),
                   jax.ShapeDtypeStruct((b, l // tl, SUBLANES, d_ff), F32),
                   jax.ShapeDtypeStruct((b, l // tl, SUBLANES, d_ff), F32)],
        scratch_shapes=[pltpu.VMEM((tb * tl, d), BF16),
                        pltpu.VMEM((tb, SUBLANES + tl, tf), F32),
                        pltpu.VMEM((tb, SUBLANES + tl, tf), F32),
                        pltpu.VMEM((nj, tb, SUBLANES, tf), F32),
                        pltpu.VMEM((nj, tb, SUBLANES, tf), F32)],
        compiler_params=_params("arbitrary", "arbitrary", "arbitrary"),
        name="conv_ffn",
    )(x, mod, norm_w.reshape(1, d), w_up, w_up, conv_w8, conv_w8, cb, cb, state8, state8, w_down)


def _pad_rows_front(x, rows):
    pad = rows - x.shape[1]
    return jnp.pad(x, ((0, 0), (pad, 0), (0, 0)))


def _layer(x, mod, conv_prev, s0, k_prev, v_prev, ffn_prev, pos0, wts, cfg):
    b, l, d = x.shape
    proj = _inproj(x, mod, wts["norm1_w"], wts["w_in"], cfg["tb"], cfg["tl_in"], cfg["tn_in"])

    nb = cfg["gdn_nb"]
    c = GDN_ROWS // nb
    o_d, s_new = _gdn(proj, _pad_rows_front(conv_prev, SUBLANES), s0, wts["conv_qkv_w8"],
                      wts["head_params"], wts["gdn_norm_w"], nb, c)

    chunk = min(CHUNK, l)
    if k_prev is None:
        o_a, kn = _swa(proj, proj, proj, True, wts["q_norm_w"], wts["k_norm_w"], wts["sinks"],
                       cfg["tq"], chunk, pos0)
    else:
        o_a, kn = _swa(proj, k_prev.reshape(b, WINDOW, KV_WIDTH), v_prev.reshape(b, WINDOW, KV_WIDTH), False,
                       wts["q_norm_w"], wts["k_norm_w"], wts["sinks"], cfg["tq"], chunk, pos0)

    x1 = _outproj(x, o_d, o_a, mod, wts["w_o"], cfg["tb"], cfg["tl_out"])
    y, last_g, last_u = _ffn(x1, mod, wts["norm2_w"], wts["w_up"], wts["ffn_conv_w8"], wts["ffn_conv_b"],
                             _pad_rows_front(ffn_prev, SUBLANES), wts["w_down"],
                             cfg["tb"], cfg["tl_ffn"], cfg["tf"])

    qkv_new = proj[:, l - (GDN_CONV - 1):, COL_QKV:COL_QKV + QKV_D_WIDTH]
    if l >= GDN_CONV - 1:
        conv_new = qkv_new
    else:
        conv_new = jnp.concatenate([conv_prev, qkv_new], axis=1)[:, -(GDN_CONV - 1):]
    v_new = proj[:, :, COL_VA:COL_VA + KV_WIDTH]
    if l >= WINDOW:
        k_cache = kn[:, l - WINDOW:]
        v_cache = v_new[:, l - WINDOW:]
    else:
        k_cache = jnp.concatenate([k_prev.reshape(b, WINDOW, KV_WIDTH)[:, l:], kn], axis=1)
        v_cache = jnp.concatenate([v_prev.reshape(b, WINDOW, KV_WIDTH)[:, l:], v_new], axis=1)
    k_cache = k_cache.reshape(b, WINDOW, SWA_KV_HEADS, HEAD_DIM)
    v_cache = v_cache.reshape(b, WINDOW, SWA_KV_HEADS, HEAD_DIM)
    ffn_new = jnp.concatenate([last_g[:, -1, SUBLANES - (FFN_CONV - 1):],
                               last_u[:, -1, SUBLANES - (FFN_CONV - 1):]], axis=-1)
    return y, conv_new, s_new, k_cache, v_cache, ffn_new


def _prep_weights(l, ada_w, ada_b, norm1_w, norm2_w, w_in, conv_qkv_w, a_log, dt_bias, gdn_norm_w,
                  q_norm_w, k_norm_w, sinks, w_o, w_up, ffn_conv_w, ffn_conv_b, w_down):
    d = w_in.shape[1]
    ab0 = QKV_D_WIDTH + GDN_WIDTH
    ab1 = ab0 + 2 * GDN_HEADS
    w = w_in[l]
    w_in_r = jnp.concatenate(
        [w[:, :ab0], w[:, ab1:], w[:, ab0:ab1], jnp.zeros((d, LANES - 2 * GDN_HEADS), w.dtype)], axis=1).astype(BF16)
    assert w_in_r.shape[1] == PROJ_WIDTH
    hp = jnp.zeros((SUBLANES, LANES), F32)
    hp = hp.at[0, :GDN_HEADS].set(a_log[l].astype(F32)).at[1, :GDN_HEADS].set(dt_bias[l].astype(F32))
    return {
        "ada_w": ada_w[l], "ada_b": ada_b[l], "norm1_w": norm1_w[l], "norm2_w": norm2_w[l],
        "w_in": w_in_r,
        "conv_qkv_w8": jnp.pad(conv_qkv_w[l], ((0, SUBLANES - GDN_CONV), (0, 0))),
        "head_params": hp,
        "gdn_norm_w": gdn_norm_w[l], "q_norm_w": q_norm_w[l], "k_norm_w": k_norm_w[l], "sinks": sinks[l],
        "w_o": w_o[l].astype(BF16), "w_up": w_up[l].astype(BF16),
        "ffn_conv_w8": jnp.pad(ffn_conv_w[l], ((0, SUBLANES - FFN_CONV), (0, 0))),
        "ffn_conv_b": ffn_conv_b[l], "w_down": w_down[l].astype(BF16),
    }


PROMPT_CFG = dict(tb=1, tl_in=1024, tn_in=1152, gdn_nb=1, tq=128, tl_out=512, tl_ffn=512, tf=512)


def _sample_cfg(b, l):
    return dict(tb=b, tl_in=l, tn_in=1152, gdn_nb=GDN_ROWS // l, tq=l, tl_out=l, tl_ffn=l, tf=512)


def kernel(x_prompt, x_sample, state_conv_qkv, state_delta, cache_swa_k, cache_swa_v, state_ffn_conv,
           c_prompt, c_sample, ada_w, ada_b, norm1_w, norm2_w, w_in, conv_qkv_w, a_log, dt_bias,
           gdn_norm_w, q_norm_w, k_norm_w, sinks, w_o, w_up, ffn_conv_w, ffn_conv_b, w_down):
    depth = w_in.shape[0]
    bp, lp, d = x_prompt.shape
    bs, ls, _ = x_sample.shape
    d_ff = w_down.shape[1]
    xp, xs = x_prompt, x_sample
    acc_p = [[] for _ in range(5)]
    acc_s = [[] for _ in range(5)]
    c_all = jnp.concatenate([c_prompt, c_sample], axis=0)
    c_rows = -(-c_all.shape[0] // SUBLANES) * SUBLANES
    c_all = jnp.pad(c_all, ((0, c_rows - c_all.shape[0]), (0, 0)))
    for l in range(depth):
        wts = _prep_weights(l, ada_w, ada_b, norm1_w, norm2_w, w_in, conv_qkv_w, a_log, dt_bias, gdn_norm_w,
                            q_norm_w, k_norm_w, sinks, w_o, w_up, ffn_conv_w, ffn_conv_b, w_down)
        mod = _modulation(c_all, wts["ada_w"], wts["ada_b"]).reshape(c_rows, N_MOD, d)
        mod_p, mod_s = mod[:bp], mod[bp:bp + bs]
        dt = xp.dtype
        zero_state = (jnp.zeros((bp, GDN_CONV - 1, QKV_D_WIDTH), dt),
                      jnp.zeros((bp, GDN_HEADS, HEAD_DIM, HEAD_DIM), dt),
                      None, None,
                      jnp.zeros((bp, FFN_CONV - 1, 2 * d_ff), dt))
        xp, *new_p = _layer(xp, mod_p, *zero_state, 0, wts, PROMPT_CFG)
        xs, *new_s = _layer(xs, mod_s, state_conv_qkv[l], state_delta[l], cache_swa_k[l], cache_swa_v[l],
                            state_ffn_conv[l], PAST_LEN, wts, _sample_cfg(bs, ls))
        for acc, t in zip(acc_p, new_p):
            acc.append(t)
        for acc, t in zip(acc_s, new_s):
            acc.append(t)
    outs_p = [jnp.stack(a) for a in acc_p]
    outs_s = [jnp.stack(a) for a in acc_s]
    return (xp, xs, *outs_p, *outs_s)
```

```python
import functools
import math

import jax
import jax.numpy as jnp
from jax import lax
from jax.experimental import pallas as pl
from jax.experimental.pallas import tpu as pltpu

F32 = jnp.float32
BF16 = jnp.bfloat16

HEAD_DIM = 128
GDN_HEADS = 8
GDN_CONV = 4
SWA_HEADS = 8
SWA_KV_HEADS = 2
SWA_GROUP = SWA_HEADS // SWA_KV_HEADS
WINDOW = 128
FFN_CONV = 3
N_MOD = 6
EPS = 1e-6
CHUNK = 64
PAST_LEN = 4096

GDN_WIDTH = GDN_HEADS * HEAD_DIM
SWA_WIDTH = SWA_HEADS * HEAD_DIM
KV_WIDTH = SWA_KV_HEADS * HEAD_DIM
QKV_D_WIDTH = 3 * GDN_WIDTH

LANES = 128
SUBLANES = 8
GDN_ROWS = 128
INV_BASE = 16
VMEM_LIMIT = 56 * 1024 * 1024
FFN_COLS = 256
FFN_ROWS = 32

COL_QKV = 0
COL_Z = QKV_D_WIDTH
COL_QA = COL_Z + GDN_WIDTH
COL_KA = COL_QA + SWA_WIDTH
COL_VA = COL_KA + KV_WIDTH
COL_AB = COL_VA + KV_WIDTH
PROJ_WIDTH = COL_AB + LANES


def _dot(a, b):
    return jnp.dot(a.astype(BF16), b.astype(BF16), preferred_element_type=F32)


def _dot_nt(a, b):
    return lax.dot_general(a.astype(BF16), b.astype(BF16), (((1,), (1,)), ((), ())),
                           preferred_element_type=F32)


def _silu(x):
    return x * jax.nn.sigmoid(x)


def _softplus(x):
    return jnp.maximum(x, 0.0) + jnp.log1p(jnp.exp(-jnp.abs(x)))


def _params(*sem):
    return pltpu.CompilerParams(dimension_semantics=sem, vmem_limit_bytes=VMEM_LIMIT)


def _mod_kernel(c_ref, w_ref, b_ref, o_ref):
    o_ref[...] = _dot(_silu(c_ref[...]), w_ref[...]) + b_ref[...]


def _modulation(c, ada_w, ada_b, tn=1024):
    rows, d = c.shape
    n = ada_w.shape[1]
    return pl.pallas_call(
        _mod_kernel,
        grid=(n // tn,),
        in_specs=[pl.BlockSpec((rows, d), lambda j: (0, 0)),
                  pl.BlockSpec((d, tn), lambda j: (0, j)),
                  pl.BlockSpec((1, tn), lambda j: (0, j))],
        out_specs=pl.BlockSpec((rows, tn), lambda j: (0, j)),
        out_shape=jax.ShapeDtypeStruct((rows, n), F32),
        compiler_params=_params("arbitrary"),
        name="modulation",
    )(c, ada_w, ada_b.reshape(1, n))


def _modulated_norm(x, nw, shift, scale):
    y = x * lax.rsqrt(jnp.mean(x * x, axis=-1, keepdims=True) + EPS) * nw
    return y * (1.0 + scale) + shift


def _inproj_kernel(x_ref, mod_ref, nw_ref, w_ref, o_ref, h_ref):
    tb, tl, d = x_ref.shape

    @pl.when(pl.program_id(2) == 0)
    def _():
        h = _modulated_norm(x_ref[...], nw_ref[...], mod_ref[:, 0:1, :], mod_ref[:, 1:2, :])
        h_ref[...] = h.reshape(tb * tl, d).astype(BF16)

    o_ref[...] = jnp.dot(h_ref[...], w_ref[...], preferred_element_type=F32).reshape(o_ref.shape)


def _inproj(x, mod, norm_w, w, tb, tl, tn):
    b, l, d = x.shape
    n = w.shape[1]
    return pl.pallas_call(
        _inproj_kernel,
        grid=(b // tb, l // tl, n // tn),
        in_specs=[pl.BlockSpec((tb, tl, d), lambda i, t, j: (i, t, 0)),
                  pl.BlockSpec((tb, N_MOD, d), lambda i, t, j: (i, 0, 0)),
                  pl.BlockSpec((1, d), lambda i, t, j: (0, 0)),
                  pl.BlockSpec((d, tn), lambda i, t, j: (0, j))],
        out_specs=pl.BlockSpec((tb, tl, tn), lambda i, t, j: (i, t, j)),
        out_shape=jax.ShapeDtypeStruct((b, l, n), F32),
        scratch_shapes=[pltpu.VMEM((tb * tl, d), BF16)],
        compiler_params=_params("arbitrary", "arbitrary", "arbitrary"),
        name="in_projection",
    )(x, mod, norm_w.reshape(1, d), w)


def _unit_lower_inverse(a_list, same_block, eye):
    sizes = sorted(same_block)
    p = [-jnp.where(same_block[sizes[0]], a, 0.0) for a in a_list]
    x = [eye + pi for pi in p]
    k = 2
    while k < sizes[0]:
        p = [_dot(pi, pi) for pi in p]
        x = [xi + _dot(xi, pi) for xi, pi in zip(x, p)]
        k *= 2
    for s_prev, s in zip(sizes[:-1], sizes[1:]):
        off = jnp.logical_and(same_block[s], jnp.logical_not(same_block[s_prev]))
        xa = [_dot(xi, jnp.where(off, a, 0.0)) for xi, a in zip(x, a_list)]
        x = [xi - _dot(xai, xi) for xi, xai in zip(x, xa)]
    return x


def _gdn_kernel(qkv_ref, z_ref, ab_ref, cst_ref, s0_ref, cw_ref, hp_ref, nw_ref,
                o_ref, sout_ref, ext_ref, s_ref):
    nb, c, _ = qkv_ref.shape
    r = nb * c
    t = pl.program_id(1)

    @pl.when(t == 0)
    def _():
        ext_ref[:, 0:SUBLANES, :] = cst_ref[...]
        s_ref[...] = s0_ref[...]

    ext_ref[:, SUBLANES:SUBLANES + c, :] = qkv_ref[...]

    ab = ab_ref[...].reshape(r, LANES)
    hp = hp_ref[...]
    g = -jnp.exp(hp[0:1, :]) * _softplus(ab + hp[1:2, :])
    beta = jax.nn.sigmoid(ab)
    pos = jnp.bitwise_and(lax.broadcasted_iota(jnp.int32, (r, LANES), 0), c - 1)
    gc = g
    s = 1
    while s < c:
        gc = gc + jnp.where(pos >= s, pltpu.roll(gc, s, 0), 0.0)
        s *= 2
    if nb == 1:
        gcl = jnp.broadcast_to(gc[r - 1:r, :], (r, LANES))
    else:
        gcl = jnp.where(pos == c - 1, gc, 0.0)
        s = 1
        while s < c:
            gcl = gcl + jnp.where(pos < c - s, pltpu.roll(gcl, r - s, 0), 0.0)
            s *= 2
    eg = jnp.exp(gc)
    egl = jnp.exp(gcl - gc)
    gl_rows = [jnp.exp(gc[sgi * c + c - 1:sgi * c + c, :]) for sgi in range(nb)]
    gct = gc.T

    row = lax.broadcasted_iota(jnp.int32, (r, r), 0)
    col = lax.broadcasted_iota(jnp.int32, (r, r), 1)

    def same(sz):
        sh = int(math.log2(sz))
        return jnp.right_shift(row, sh) == jnp.right_shift(col, sh)

    sizes = []
    sz = min(INV_BASE, c)
    while sz <= c:
        sizes.append(sz)
        sz *= 2
    same_block = {sz: same(sz) for sz in sizes}
    seg = same_block[c]
    incl = jnp.logical_and(seg, row >= col)
    strict = jnp.logical_and(seg, row > col)
    eye = (row == col).astype(F32)
    colseg = jnp.right_shift(lax.broadcasted_iota(jnp.int32, (HEAD_DIM, r), 1), int(math.log2(c)))

    cw = cw_ref[...]

    def conv_cols(c0):
        acc = cw[0:1, c0:c0 + HEAD_DIM] * ext_ref[:, SUBLANES - 3:SUBLANES - 3 + c, c0:c0 + HEAD_DIM]
        for i in range(1, GDN_CONV):
            lo = SUBLANES - 3 + i
            acc = acc + cw[i:i + 1, c0:c0 + HEAD_DIM] * ext_ref[:, lo:lo + c, c0:c0 + HEAD_DIM]
        return _silu(acc).reshape(r, HEAD_DIM)

    nw = nw_ref[...]
    heads = range(GDN_HEADS)
    q = [conv_cols(h * HEAD_DIM) for h in heads]
    k = [conv_cols(GDN_WIDTH + h * HEAD_DIM) for h in heads]
    v = [conv_cols(2 * GDN_WIDTH + h * HEAD_DIM) for h in heads]
    qn = [x * (lax.rsqrt(jnp.sum(x * x, axis=-1, keepdims=True) + EPS) * (HEAD_DIM ** -0.5)) for x in q]
    kn = [x * lax.rsqrt(jnp.sum(x * x, axis=-1, keepdims=True) + EPS) for x in k]
    bcol = [beta[:, SUBLANES + h:SUBLANES + h + 1] for h in heads]
    egcol = [eg[:, h:h + 1] for h in heads]
    dec = [jnp.exp(jnp.where(incl, gc[:, h:h + 1] - gct[h:h + 1, :], -jnp.inf)) for h in heads]
    kb = [kn[h] * bcol[h] for h in heads]
    kn16 = [x.astype(BF16) for x in kn]
    kq = [_dot_nt(jnp.concatenate([kb[h].astype(BF16), qn[h].astype(BF16)], axis=0), kn16[h]) for h in heads]
    a = [jnp.where(strict, kq[h][:r] * dec[h], 0.0) for h in heads]
    qk16 = [(kq[h][r:] * dec[h]).astype(BF16) for h in heads]
    tinv = _unit_lower_inverse(a, same_block, eye)
    sol = [_dot(tinv[h], jnp.concatenate([v[h] * bcol[h], kb[h] * egcol[h]], axis=1)) for h in heads]
    qg = [qn[h] * egcol[h] for h in heads]
    kdt16 = [(kn[h] * egl[:, h:h + 1]).T.astype(BF16) for h in heads]

    ws, qs = [], []
    for h in heads:
        w = sol[h][:, HEAD_DIM:]
        ws_parts, qs_parts = [], []
        for sgi in range(nb):
            lo = sgi * c
            wq = jnp.concatenate([w[lo:lo + c], qg[h][lo:lo + c]], axis=0)
            res = _dot(wq, s_ref[sgi, h])
            ws_parts.append(res[:c])
            qs_parts.append(res[c:])
        ws.append(ws_parts[0] if nb == 1 else jnp.concatenate(ws_parts, axis=0))
        qs.append(qs_parts[0] if nb == 1 else jnp.concatenate(qs_parts, axis=0))
    v16 = [(sol[h][:, :HEAD_DIM] - ws[h]).astype(BF16) for h in heads]
    o = [qs[h] + jnp.dot(qk16[h], v16[h], preferred_element_type=F32) for h in heads]
    for h in heads:
        for sgi in range(nb):
            if nb == 1:
                kdm = kdt16[h]
            else:
                kdm = jnp.where(colseg == sgi, kdt16[h], jnp.zeros_like(kdt16[h]))
            s_ref[sgi, h] = (s_ref[sgi, h] * gl_rows[sgi][:, h:h + 1]
                             + jnp.dot(kdm, v16[h], preferred_element_type=F32))
    for h in heads:
        on = o[h] * lax.rsqrt(jnp.mean(o[h] * o[h], axis=-1, keepdims=True) + EPS) * nw
        zh = z_ref[:, :, h * HEAD_DIM:(h + 1) * HEAD_DIM].reshape(r, HEAD_DIM)
        out = on * _silu(zh)
        o_ref[:, :, h * HEAD_DIM:(h + 1) * HEAD_DIM] = out.reshape(nb, c, HEAD_DIM).astype(o_ref.dtype)

    ext_ref[:, 0:SUBLANES, :] = ext_ref[:, c:c + SUBLANES, :]

    @pl.when(t == pl.num_programs(1) - 1)
    def _():
        sout_ref[...] = s_ref[...]


def _gdn(proj, conv_state8, s0, conv_w8, head_params, norm_w, nb, c):
    b, l, _ = proj.shape
    nt = l // c
    zb = COL_Z // GDN_WIDTH
    abb = COL_AB // LANES
    return pl.pallas_call(
        _gdn_kernel,
        grid=(b // nb, nt),
        in_specs=[pl.BlockSpec((nb, c, QKV_D_WIDTH), lambda i, t: (i, t, 0)),
                  pl.BlockSpec((nb, c, GDN_WIDTH), lambda i, t: (i, t, zb)),
                  pl.BlockSpec((nb, c, LANES), lambda i, t: (i, t, abb)),
                  pl.BlockSpec((nb, SUBLANES, QKV_D_WIDTH), lambda i, t: (i, 0, 0)),
                  pl.BlockSpec((nb, GDN_HEADS, HEAD_DIM, HEAD_DIM), lambda i, t: (i, 0, 0, 0)),
                  pl.BlockSpec((SUBLANES, QKV_D_WIDTH), lambda i, t: (0, 0)),
                  pl.BlockSpec((SUBLANES, LANES), lambda i, t: (0, 0)),
                  pl.BlockSpec((1, HEAD_DIM), lambda i, t: (0, 0))],
        out_specs=[pl.BlockSpec((nb, c, GDN_WIDTH), lambda i, t: (i, t, 0)),
                   pl.BlockSpec((nb, GDN_HEADS, HEAD_DIM, HEAD_DIM), lambda i, t: (i, 0, 0, 0))],
        out_shape=[jax.ShapeDtypeStruct((b, l, GDN_WIDTH), BF16),
                   jax.ShapeDtypeStruct((b, GDN_HEADS, HEAD_DIM, HEAD_DIM), F32)],
        scratch_shapes=[pltpu.VMEM((nb, SUBLANES + c, QKV_D_WIDTH), F32),
                        pltpu.VMEM((nb, GDN_HEADS, HEAD_DIM, HEAD_DIM), F32)],
        compiler_params=_params("arbitrary", "arbitrary"),
        name="gated_delta_rule",
    )(proj, proj, proj, conv_state8, s0, conv_w8, head_params, norm_w.reshape(1, HEAD_DIM))


def _head_rms(x, w):
    return x * lax.rsqrt(jnp.mean(x * x, axis=-1, keepdims=True) + EPS) * w


def _swa_kernel(sink_ref, q_ref, k_ref, v_ref, kp_ref, vp_ref, qw_ref, kw_ref, o_ref, kn_ref,
                *, chunk, pos0, norm_prev):
    tq = q_ref.shape[1]
    tk = max(tq, LANES)
    m = SWA_GROUP * tq
    t0 = pos0 + pl.program_id(1) * tq
    qw = qw_ref[...]
    kw = kw_ref[...]

    rown = lax.broadcasted_iota(jnp.int32, (m, 1), 0)
    grp = rown // tq
    qi = rown - grp * tq
    cs = (qi // chunk) * chunk
    kj = lax.broadcasted_iota(jnp.int32, (1, tk), 1)
    pj = lax.broadcasted_iota(jnp.int32, (1, WINDOW), 1)
    vis_own = jnp.logical_and(kj < cs + chunk, kj < tq)
    vis_prev = jnp.logical_and(pj - WINDOW >= cs - WINDOW, t0 - WINDOW + pj >= 0)
    dist_own = jnp.abs(qi - kj).astype(F32)
    dist_prev = (qi + WINDOW - pj).astype(F32)
    scale = HEAD_DIM ** -0.5

    for kvh in range(SWA_KV_HEADS):
        slope = jnp.zeros((m, 1), F32)
        sink = jnp.zeros((m, 1), F32)
        for gi in range(SWA_GROUP):
            hq = kvh * SWA_GROUP + gi
            slope = jnp.where(grp == gi, 2.0 ** (-8.0 * (hq + 1) / SWA_HEADS), slope)
            sink = jnp.where(grp == gi, sink_ref[hq], sink)
        qs = jnp.concatenate(
            [_head_rms(q_ref[0, :, (kvh * SWA_GROUP + gi) * HEAD_DIM:(kvh * SWA_GROUP + gi + 1) * HEAD_DIM], qw)
             for gi in range(SWA_GROUP)], axis=0)
        ksl = slice(kvh * HEAD_DIM, (kvh + 1) * HEAD_DIM)
        k_own = _head_rms(k_ref[0, :, ksl], kw)
        kn_ref[0, :, ksl] = k_own
        v_own = v_ref[0, :, ksl]
        if tk > tq:
            pad = jnp.zeros((tk - tq, HEAD_DIM), F32)
            k_own = jnp.concatenate([k_own, pad], axis=0)
            v_own = jnp.concatenate([v_own, pad], axis=0)
        k_prev = kp_ref[0, :, ksl]
        if norm_prev:
            k_prev = _head_rms(k_prev, kw)
        v_prev = vp_ref[0, :, ksl]

        s_own = jnp.where(vis_own, _dot_nt(qs, k_own) * scale - slope * dist_own, -jnp.inf)
        s_prev = jnp.where(vis_prev, _dot_nt(qs, k_prev) * scale - slope * dist_prev, -jnp.inf)
        mx = jnp.maximum(jnp.maximum(jnp.max(s_own, axis=-1, keepdims=True),
                                     jnp.max(s_prev, axis=-1, keepdims=True)), sink)
        p_own = jnp.exp(s_own - mx)
        p_prev = jnp.exp(s_prev - mx)
        den = (jnp.sum(p_own, axis=-1, keepdims=True) + jnp.sum(p_prev, axis=-1, keepdims=True)
               + jnp.exp(sink - mx))
        inv = 1.0 / den
        o = _dot(p_own * inv, v_own) + _dot(p_prev * inv, v_prev)
        for gi in range(SWA_GROUP):
            hq = kvh * SWA_GROUP + gi
            o_ref[0, :, hq * HEAD_DIM:(hq + 1) * HEAD_DIM] = o[gi * tq:(gi + 1) * tq].astype(o_ref.dtype)


def _swa(proj, k_prev_src, v_prev_src, prev_from_proj, q_norm_w, k_norm_w, sinks, tq, chunk, pos0):
    b, l, _ = proj.shape
    qb = COL_QA // SWA_WIDTH
    kb = COL_KA // KV_WIDTH
    vb = COL_VA // KV_WIDTH
    if prev_from_proj:
        assert tq == WINDOW
        kp_spec = pl.BlockSpec((1, WINDOW, KV_WIDTH), lambda i, t: (i, jnp.maximum(t - 1, 0), kb))
        vp_spec = pl.BlockSpec((1, WINDOW, KV_WIDTH), lambda i, t: (i, jnp.maximum(t - 1, 0), vb))
    else:
        assert l == tq
        kp_spec = pl.BlockSpec((1, WINDOW, KV_WIDTH), lambda i, t: (i, 0, 0))
        vp_spec = pl.BlockSpec((1, WINDOW, KV_WIDTH), lambda i, t: (i, 0, 0))
    kernel = functools.partial(_swa_kernel, chunk=chunk, pos0=pos0, norm_prev=prev_from_proj)
    return pl.pallas_call(
        kernel,
        grid=(b, l // tq),
        in_specs=[pl.BlockSpec(memory_space=pltpu.SMEM),
                  pl.BlockSpec((1, tq, SWA_WIDTH), lambda i, t: (i, t, qb)),
                  pl.BlockSpec((1, tq, KV_WIDTH), lambda i, t: (i, t, kb)),
                  pl.BlockSpec((1, tq, KV_WIDTH), lambda i, t: (i, t, vb)),
                  kp_spec, vp_spec,
                  pl.BlockSpec((1, HEAD_DIM), lambda i, t: (0, 0)),
                  pl.BlockSpec((1, HEAD_DIM), lambda i, t: (0, 0))],
        out_specs=[pl.BlockSpec((1, tq, SWA_WIDTH), lambda i, t: (i, t, 0)),
                   pl.BlockSpec((1, tq, KV_WIDTH), lambda i, t: (i, t, 0))],
        out_shape=[jax.ShapeDtypeStruct((b, l, SWA_WIDTH), BF16),
                   jax.ShapeDtypeStruct((b, l, KV_WIDTH), F32)],
        compiler_params=_params("arbitrary", "arbitrary"),
        name="sliding_window_attention",
    )(sinks, proj, proj, proj, k_prev_src, v_prev_src,
      q_norm_w.reshape(1, HEAD_DIM), k_norm_w.reshape(1, HEAD_DIM))


def _outproj_kernel(x_ref, od_ref, oa_ref, mod_ref, w_ref, o_ref):
    tb, tl, d = x_ref.shape
    od = od_ref[...].reshape(tb * tl, GDN_WIDTH)
    oa = oa_ref[...].reshape(tb * tl, SWA_WIDTH)
    acc = (jnp.dot(od, w_ref[0:GDN_WIDTH, :], preferred_element_type=F32)
           + jnp.dot(oa, w_ref[GDN_WIDTH:GDN_WIDTH + SWA_WIDTH, :], preferred_element_type=F32))
    o_ref[...] = x_ref[...] + mod_ref[:, 2:3, :] * acc.reshape(tb, tl, d)


def _outproj(x, o_d, o_a, mod, w_o, tb, tl):
    b, l, d = x.shape
    return pl.pallas_call(
        _outproj_kernel,
        grid=(b // tb, l // tl),
        in_specs=[pl.BlockSpec((tb, tl, d), lambda i, t: (i, t, 0)),
                  pl.BlockSpec((tb, tl, GDN_WIDTH), lambda i, t: (i, t, 0)),
                  pl.BlockSpec((tb, tl, SWA_WIDTH), lambda i, t: (i, t, 0)),
                  pl.BlockSpec((tb, N_MOD, d), lambda i, t: (i, 0, 0)),
                  pl.BlockSpec(w_o.shape, lambda i, t: (0, 0))],
        out_specs=pl.BlockSpec((tb, tl, d), lambda i, t: (i, t, 0)),
        out_shape=jax.ShapeDtypeStruct((b, l, d), F32),
        compiler_params=_params("arbitrary", "arbitrary"),
        name="out_projection",
    )(x, o_d, o_a, mod, w_o)


def _ffn_kernel(x_ref, mod_ref, nw_ref, wg_ref, wu_ref, cwg_ref, cwu_ref, bg_ref, bu_ref,
                stg_ref, stu_ref, wd_ref, y_ref, lastg_ref, lastu_ref,
                h_ref, act_ref, extg_ref, extu_ref, carg_ref, caru_ref):
    tb, tl, d = x_ref.shape
    tf = wg_ref.shape[1]
    t = pl.program_id(1)
    j = pl.program_id(2)

    halves = ((wg_ref, cwg_ref, bg_ref, stg_ref, extg_ref, carg_ref, lastg_ref),
              (wu_ref, cwu_ref, bu_ref, stu_ref, extu_ref, caru_ref, lastu_ref))

    @pl.when(j == 0)
    def _():
        h = _modulated_norm(x_ref[...], nw_ref[...], mod_ref[:, 3:4, :], mod_ref[:, 4:5, :])
        h_ref[...] = h.reshape(tb * tl, d).astype(BF16)
        y_ref[...] = jnp.zeros_like(y_ref)

    @pl.when(t == 0)
    def _():
        for _, _, _, st_ref, ext_ref, _, _ in halves:
            ext_ref[:, 0:SUBLANES, :] = st_ref[...]

    @pl.when(t > 0)
    def _():
        for _, _, _, _, ext_ref, car_ref, _ in halves:
            ext_ref[:, 0:SUBLANES, :] = car_ref[j]

    h = h_ref[...]
    ncb = tf // FFN_COLS
    for cb in range(ncb):
        cs = slice(cb * FFN_COLS, (cb + 1) * FFN_COLS)
        for w_ref, _, _, _, ext_ref, _, _ in halves:
            u = jnp.dot(h, w_ref[:, cs], preferred_element_type=F32)
            ext_ref[:, SUBLANES:SUBLANES + tl, cs] = u.reshape(tb, tl, FFN_COLS)
    for _, _, _, _, ext_ref, car_ref, last_ref in halves:
        tail = ext_ref[:, tl:tl + SUBLANES, :]
        car_ref[j] = tail
        last_ref[:, 0] = tail

    if tb == 1:
        chunks = [(slice(0, 1), r0, FFN_ROWS) for r0 in range(0, tl, FFN_ROWS)]
    else:
        nbc = max(1, FFN_ROWS // tl)
        chunks = [(slice(b0, b0 + nbc), 0, tl) for b0 in range(0, tb, nbc)]
    for cb in range(ncb):
        cs = slice(cb * FFN_COLS, (cb + 1) * FFN_COLS)
        for bs, r0, nr in chunks:
            conv = []
            for _, cw_ref, b_ref, _, ext_ref, _, _ in halves:
                acc = b_ref[:, cs]
                for i in range(FFN_CONV):
                    lo = SUBLANES - (FFN_CONV - 1) + i + r0
                    acc = acc + cw_ref[i:i + 1, cs] * ext_ref[bs, lo:lo + nr, cs]
                conv.append(acc)
            act = _silu(conv[0]) * conv[1]
            nbs = bs.stop - bs.start
            row0 = bs.start * tl + r0
            act_ref[row0:row0 + nbs * nr, cs] = act.reshape(nbs * nr, FFN_COLS).astype(BF16)

    y_ref[...] += jnp.dot(act_ref[...], wd_ref[...], preferred_element_type=F32).reshape(tb, tl, d)

    @pl.when(j == pl.num_programs(2) - 1)
    def _():
        y_ref[...] = x_ref[...] + mod_ref[:, 5:6, :] * y_ref[...]


def _ffn(x, mod, norm_w, w_up, conv_w8, conv_b, state8, w_down, tb, tl, tf):
    b, l, d = x.shape
    d_ff = w_down.shape[0]
    nj = d_ff // tf
    cb = conv_b.reshape(1, 2 * d_ff)
    lo = lambda i, t, j: (0, j)
    hi = lambda i, t, j: (0, nj + j)
    return pl.pallas_call(
        _ffn_kernel,
        grid=(b // tb, l // tl, nj),
        in_specs=[pl.BlockSpec((tb, tl, d), lambda i, t, j: (i, t, 0)),
                  pl.BlockSpec((tb, N_MOD, d), lambda i, t, j: (i, 0, 0)),
                  pl.BlockSpec((1, d), lambda i, t, j: (0, 0)),
                  pl.BlockSpec((d, tf), lo), pl.BlockSpec((d, tf), hi),
                  pl.BlockSpec((SUBLANES, tf), lo), pl.BlockSpec((SUBLANES, tf), hi),
                  pl.BlockSpec((1, tf), lo), pl.BlockSpec((1, tf), hi),
                  pl.BlockSpec((tb, SUBLANES, tf), lambda i, t, j: (i, 0, j)),
                  pl.BlockSpec((tb, SUBLANES, tf), lambda i, t, j: (i, 0, nj + j)),
                  pl.BlockSpec((tf, d), lambda i, t, j: (j, 0))],
        out_specs=[pl.BlockSpec((tb, tl, d), lambda i, t, j: (i, t, 0)),
                   pl.BlockSpec((tb, 1, SUBLANES, tf), lambda i, t, j: (i, t, 0, j)),
                   pl.BlockSpec((tb, 1, SUBLANES, tf), lambda i, t, j: (i, t, 0, j))],
        out_shape=[jax.ShapeDtypeStruct((b, l, d), F32),
                   jax.ShapeDtypeStruct((b, l // tl, SUBLANES, d_ff), F32),
                   jax.ShapeDtypeStruct((b, l // tl, SUBLANES, d_ff), F32)],
        scratch_shapes=[pltpu.VMEM((tb * tl, d), BF16),
                        pltpu.VMEM((tb * tl, tf), BF16),
                        pltpu.VMEM((tb, SUBLANES + tl, tf), F32),
                        pltpu.VMEM((tb, SUBLANES + tl, tf), F32),
                        pltpu.VMEM((nj, tb, SUBLANES, tf), F32),
                        pltpu.VMEM((nj, tb, SUBLANES, tf), F32)],
        compiler_params=_params("arbitrary", "arbitrary", "arbitrary"),
        name="conv_ffn",
    )(x, mod, norm_w.reshape(1, d), w_up, w_up, conv_w8, conv_w8, cb, cb, state8, state8, w_down)


def _pad_rows_front(x, rows):
    pad = rows - x.shape[1]
    return jnp.pad(x, ((0, 0), (pad, 0), (0, 0)))


def _layer(x, mod, conv_prev, s0, k_prev, v_prev, ffn_prev, pos0, wts, cfg):
    b, l, d = x.shape
    proj = _inproj(x, mod, wts["norm1_w"], wts["w_in"], cfg["tb"], cfg["tl_in"], cfg["tn_in"])

    nb = cfg["gdn_nb"]
    c = GDN_ROWS // nb
    o_d, s_new = _gdn(proj, _pad_rows_front(conv_prev, SUBLANES), s0, wts["conv_qkv_w8"],
                      wts["head_params"], wts["gdn_norm_w"], nb, c)

    chunk = min(CHUNK, l)
    if k_prev is None:
        o_a, kn = _swa(proj, proj, proj, True, wts["q_norm_w"], wts["k_norm_w"], wts["sinks"],
                       cfg["tq"], chunk, pos0)
    else:
        o_a, kn = _swa(proj, k_prev.reshape(b, WINDOW, KV_WIDTH), v_prev.reshape(b, WINDOW, KV_WIDTH), False,
                       wts["q_norm_w"], wts["k_norm_w"], wts["sinks"], cfg["tq"], chunk, pos0)

    x1 = _outproj(x, o_d, o_a, mod, wts["w_o"], cfg["tb"], cfg["tl_out"])
    y, last_g, last_u = _ffn(x1, mod, wts["norm2_w"], wts["w_up"], wts["ffn_conv_w8"], wts["ffn_conv_b"],
                             _pad_rows_front(ffn_prev, SUBLANES), wts["w_down"],
                             cfg["tb"], cfg["tl_ffn"], cfg["tf"])

    qkv_new = proj[:, l - (GDN_CONV - 1):, COL_QKV:COL_QKV + QKV_D_WIDTH]
    if l >= GDN_CONV - 1:
        conv_new = qkv_new
    else:
        conv_new = jnp.concatenate([conv_prev, qkv_new], axis=1)[:, -(GDN_CONV - 1):]
    v_new = proj[:, :, COL_VA:COL_VA + KV_WIDTH]
    if l >= WINDOW:
        k_cache = kn[:, l - WINDOW:]
        v_cache = v_new[:, l - WINDOW:]
    else:
        k_cache = jnp.concatenate([k_prev.reshape(b, WINDOW, KV_WIDTH)[:, l:], kn], axis=1)
        v_cache = jnp.concatenate([v_prev.reshape(b, WINDOW, KV_WIDTH)[:, l:], v_new], axis=1)
    k_cache = k_cache.reshape(b, WINDOW, SWA_KV_HEADS, HEAD_DIM)
    v_cache = v_cache.reshape(b, WINDOW, SWA_KV_HEADS, HEAD_DIM)
    ffn_new = jnp.concatenate([last_g[:, -1, SUBLANES - (FFN_CONV - 1):],
                               last_u[:, -1, SUBLANES - (FFN_CONV - 1):]], axis=-1)
    return y, conv_new, s_new, k_cache, v_cache, ffn_new


def _prep_weights(l, ada_w, ada_b, norm1_w, norm2_w, w_in, conv_qkv_w, a_log, dt_bias, gdn_norm_w,
                  q_norm_w, k_norm_w, sinks, w_o, w_up, ffn_conv_w, ffn_conv_b, w_down):
    d = w_in.shape[1]
    ab0 = QKV_D_WIDTH + GDN_WIDTH
    ab1 = ab0 + 2 * GDN_HEADS
    w = w_in[l]
    w_in_r = jnp.concatenate(
        [w[:, :ab0], w[:, ab1:], w[:, ab0:ab1], jnp.zeros((d, LANES - 2 * GDN_HEADS), w.dtype)], axis=1).astype(BF16)
    assert w_in_r.shape[1] == PROJ_WIDTH
    hp = jnp.zeros((SUBLANES, LANES), F32)
    hp = hp.at[0, :GDN_HEADS].set(a_log[l].astype(F32)).at[1, :GDN_HEADS].set(dt_bias[l].astype(F32))
    return {
        "ada_w": ada_w[l], "ada_b": ada_b[l], "norm1_w": norm1_w[l], "norm2_w": norm2_w[l],
        "w_in": w_in_r,
        "conv_qkv_w8": jnp.pad(conv_qkv_w[l], ((0, SUBLANES - GDN_CONV), (0, 0))),
        "head_params": hp,
        "gdn_norm_w": gdn_norm_w[l], "q_norm_w": q_norm_w[l], "k_norm_w": k_norm_w[l], "sinks": sinks[l],
        "w_o": w_o[l].astype(BF16), "w_up": w_up[l].astype(BF16),
        "ffn_conv_w8": jnp.pad(ffn_conv_w[l], ((0, SUBLANES - FFN_CONV), (0, 0))),
        "ffn_conv_b": ffn_conv_b[l], "w_down": w_down[l].astype(BF16),
    }


PROMPT_CFG = dict(tb=1, tl_in=1024, tn_in=1152, gdn_nb=1, tq=128, tl_out=512, tl_ffn=512, tf=512)


def _sample_cfg(b, l):
    return dict(tb=b, tl_in=l, tn_in=1152, gdn_nb=GDN_ROWS // l, tq=l, tl_out=l, tl_ffn=l, tf=512)


def kernel(x_prompt, x_sample, state_conv_qkv, state_delta, cache_swa_k, cache_swa_v, state_ffn_conv,
           c_prompt, c_sample, ada_w, ada_b, norm1_w, norm2_w, w_in, conv_qkv_w, a_log, dt_bias,
           gdn_norm_w, q_norm_w, k_norm_w, sinks, w_o, w_up, ffn_conv_w, ffn_conv_b, w_down):
    depth = w_in.shape[0]
    bp, lp, d = x_prompt.shape
    bs, ls, _ = x_sample.shape
    d_ff = w_down.shape[1]
    xp, xs = x_prompt, x_sample
    acc_p = [[] for _ in range(5)]
    acc_s = [[] for _ in range(5)]
    c_all = jnp.concatenate([c_prompt, c_sample], axis=0)
    c_rows = -(-c_all.shape[0] // SUBLANES) * SUBLANES
    c_all = jnp.pad(c_all, ((0, c_rows - c_all.shape[0]), (0, 0)))
    for l in range(depth):
        wts = _prep_weights(l, ada_w, ada_b, norm1_w, norm2_w, w_in, conv_qkv_w, a_log, dt_bias, gdn_norm_w,
                            q_norm_w, k_norm_w, sinks, w_o, w_up, ffn_conv_w, ffn_conv_b, w_down)
        mod = _modulation(c_all, wts["ada_w"], wts["ada_b"]).reshape(c_rows, N_MOD, d)
        mod_p, mod_s = mod[:bp], mod[bp:bp + bs]
        dt = xp.dtype
        zero_state = (jnp.zeros((bp, GDN_CONV - 1, QKV_D_WIDTH), dt),
                      jnp.zeros((bp, GDN_HEADS, HEAD_DIM, HEAD_DIM), dt),
                      None, None,
                      jnp.zeros((bp, FFN_CONV - 1, 2 * d_ff), dt))
        xp, *new_p = _layer(xp, mod_p, *zero_state, 0, wts, PROMPT_CFG)
        xs, *new_s = _layer(xs, mod_s, state_conv_qkv[l], state_delta[l], cache_swa_k[l], cache_swa_v[l],
                            state_ffn_conv[l], PAST_LEN, wts, _sample_cfg(bs, ls))
        for acc, t in zip(acc_p, new_p):
            acc.append(t)
        for acc, t in zip(acc_s, new_s):
            acc.append(t)
    outs_p = [jnp.stack(a) for a in acc_p]
    outs_s = [jnp.stack(a) for a in acc_s]
    return (xp, xs, *outs_p, *outs_s)
```

```python
import functools
import math

import jax
import jax.numpy as jnp
from jax import lax
from jax.experimental import pallas as pl
from jax.experimental.pallas import tpu as pltpu

F32 = jnp.float32
BF16 = jnp.bfloat16

HEAD_DIM = 128
GDN_HEADS = 8
GDN_CONV = 4
SWA_HEADS = 8
SWA_KV_HEADS = 2
SWA_GROUP = SWA_HEADS // SWA_KV_HEADS
WINDOW = 128
FFN_CONV = 3
N_MOD = 6
EPS = 1e-6
CHUNK = 64
PAST_LEN = 4096

GDN_WIDTH = GDN_HEADS * HEAD_DIM
SWA_WIDTH = SWA_HEADS * HEAD_DIM
KV_WIDTH = SWA_KV_HEADS * HEAD_DIM
QKV_D_WIDTH = 3 * GDN_WIDTH

LANES = 128
SUBLANES = 8
GDN_ROWS = 128
INV_BASE = 16
VMEM_LIMIT = 56 * 1024 * 1024
FFN_COLS = 256
FFN_ROWS = 32

COL_QKV = 0
COL_Z = QKV_D_WIDTH
COL_QA = COL_Z + GDN_WIDTH
COL_KA = COL_QA + SWA_WIDTH
COL_VA = COL_KA + KV_WIDTH
COL_AB = COL_VA + KV_WIDTH
PROJ_WIDTH = COL_AB + LANES


def _dot(a, b):
    return jnp.dot(a.astype(BF16), b.astype(BF16), preferred_element_type=F32)


def _dot_nt(a, b):
    return lax.dot_general(a.astype(BF16), b.astype(BF16), (((1,), (1,)), ((), ())),
                           preferred_element_type=F32)


def _silu(x):
    return x * jax.nn.sigmoid(x)


def _softplus(x):
    return jnp.maximum(x, 0.0) + jnp.log1p(jnp.exp(-jnp.abs(x)))


def _params(*sem):
    return pltpu.CompilerParams(dimension_semantics=sem, vmem_limit_bytes=VMEM_LIMIT)


def _mod_kernel(c_ref, w_ref, b_ref, o_ref):
    o_ref[...] = _dot(_silu(c_ref[...]), w_ref[...]) + b_ref[...]


def _modulation(c, ada_w, ada_b, tn=1024):
    rows, d = c.shape
    n = ada_w.shape[1]
    return pl.pallas_call(
        _mod_kernel,
        grid=(n // tn,),
        in_specs=[pl.BlockSpec((rows, d), lambda j: (0, 0)),
                  pl.BlockSpec((d, tn), lambda j: (0, j)),
                  pl.BlockSpec((1, tn), lambda j: (0, j))],
        out_specs=pl.BlockSpec((rows, tn), lambda j: (0, j)),
        out_shape=jax.ShapeDtypeStruct((rows, n), F32),
        compiler_params=_params("arbitrary"),
        name="modulation",
    )(c, ada_w, ada_b.reshape(1, n))


def _modulated_norm(x, nw, shift, scale):
    y = x * lax.rsqrt(jnp.mean(x * x, axis=-1, keepdims=True) + EPS) * nw
    return y * (1.0 + scale) + shift


def _inproj_kernel(x_ref, mod_ref, nw_ref, w_ref, o_ref, h_ref):
    tb, tl, d = x_ref.shape

    @pl.when(pl.program_id(2) == 0)
    def _():
        h = _modulated_norm(x_ref[...], nw_ref[...], mod_ref[:, 0:1, :], mod_ref[:, 1:2, :])
        h_ref[...] = h.reshape(tb * tl, d).astype(BF16)

    o_ref[...] = jnp.dot(h_ref[...], w_ref[...], preferred_element_type=F32).reshape(o_ref.shape)


def _inproj(x, mod, norm_w, w, tb, tl, tn):
    b, l, d = x.shape
    n = w.shape[1]
    return pl.pallas_call(
        _inproj_kernel,
        grid=(b // tb, l // tl, n // tn),
        in_specs=[pl.BlockSpec((tb, tl, d), lambda i, t, j: (i, t, 0)),
                  pl.BlockSpec((tb, N_MOD, d), lambda i, t, j: (i, 0, 0)),
                  pl.BlockSpec((1, d), lambda i, t, j: (0, 0)),
                  pl.BlockSpec((d, tn), lambda i, t, j: (0, j))],
        out_specs=pl.BlockSpec((tb, tl, tn), lambda i, t, j: (i, t, j)),
        out_shape=jax.ShapeDtypeStruct((b, l, n), F32),
        scratch_shapes=[pltpu.VMEM((tb * tl, d), BF16)],
        compiler_params=_params("arbitrary", "arbitrary", "arbitrary"),
        name="in_projection",
    )(x, mod, norm_w.reshape(1, d), w)


def _unit_lower_inverse(a_list, same_block, eye):
    sizes = sorted(same_block)
    p = [-jnp.where(same_block[sizes[0]], a, 0.0) for a in a_list]
    x = [eye + pi for pi in p]
    k = 2
    while k < sizes[0]:
        p = [_dot(pi, pi) for pi in p]
        x = [xi + _dot(xi, pi) for xi, pi in zip(x, p)]
        k *= 2
    for s_prev, s in zip(sizes[:-1], sizes[1:]):
        off = jnp.logical_and(same_block[s], jnp.logical_not(same_block[s_prev]))
        xa = [_dot(xi, jnp.where(off, a, 0.0)) for xi, a in zip(x, a_list)]
        x = [xi - _dot(xai, xi) for xi, xai in zip(x, xa)]
    return x


def _gdn_kernel(qkv_ref, z_ref, ab_ref, cst_ref, s0_ref, cw_ref, hp_ref, nw_ref,
                o_ref, sout_ref, ext_ref, s_ref):
    nb, c, _ = qkv_ref.shape
    r = nb * c
    t = pl.program_id(1)

    @pl.when(t == 0)
    def _():
        ext_ref[:, 0:SUBLANES, :] = cst_ref[...]
        s_ref[...] = s0_ref[...]

    ext_ref[:, SUBLANES:SUBLANES + c, :] = qkv_ref[...]

    ab = ab_ref[...].reshape(r, LANES)
    hp = hp_ref[...]
    g = -jnp.exp(hp[0:1, :]) * _softplus(ab + hp[1:2, :])
    beta = jax.nn.sigmoid(ab)
    pos = jnp.bitwise_and(lax.broadcasted_iota(jnp.int32, (r, LANES), 0), c - 1)
    gc = g
    s = 1
    while s < c:
        gc = gc + jnp.where(pos >= s, pltpu.roll(gc, s, 0), 0.0)
        s *= 2
    if nb == 1:
        gcl = jnp.broadcast_to(gc[r - 1:r, :], (r, LANES))
    else:
        gcl = jnp.where(pos == c - 1, gc, 0.0)
        s = 1
        while s < c:
            gcl = gcl + jnp.where(pos < c - s, pltpu.roll(gcl, r - s, 0), 0.0)
            s *= 2
    eg = jnp.exp(gc)
    egl = jnp.exp(gcl - gc)
    gl_rows = [jnp.exp(gc[sgi * c + c - 1:sgi * c + c, :]) for sgi in range(nb)]
    gct = gc.T

    row = lax.broadcasted_iota(jnp.int32, (r, r), 0)
    col = lax.broadcasted_iota(jnp.int32, (r, r), 1)

    def same(sz):
        sh = int(math.log2(sz))
        return jnp.right_shift(row, sh) == jnp.right_shift(col, sh)

    sizes = []
    sz = min(INV_BASE, c)
    while sz <= c:
        sizes.append(sz)
        sz *= 2
    same_block = {sz: same(sz) for sz in sizes}
    seg = same_block[c]
    incl = jnp.logical_and(seg, row >= col)
    strict = jnp.logical_and(seg, row > col)
    eye = (row == col).astype(F32)
    colseg = jnp.right_shift(lax.broadcasted_iota(jnp.int32, (HEAD_DIM, r), 1), int(math.log2(c)))

    cw = cw_ref[...]

    def conv_cols(c0):
        cols = slice(c0, c0 + HEAD_DIM)
        if nb == 1:
            full = ext_ref[0, :, cols]
            acc = cw[GDN_CONV - 1:GDN_CONV, cols] * full[SUBLANES:]
            for i in range(GDN_CONV - 1):
                acc = acc + cw[i:i + 1, cols] * pltpu.roll(full, GDN_CONV - 1 - i, 0)[SUBLANES:]
            return _silu(acc)
        acc = cw[0:1, cols] * ext_ref[:, SUBLANES - 3:SUBLANES - 3 + c, cols]
        for i in range(1, GDN_CONV):
            lo = SUBLANES - 3 + i
            acc = acc + cw[i:i + 1, cols] * ext_ref[:, lo:lo + c, cols]
        return _silu(acc).reshape(r, HEAD_DIM)

    nw = nw_ref[...]
    heads = range(GDN_HEADS)
    q = [conv_cols(h * HEAD_DIM) for h in heads]
    k = [conv_cols(GDN_WIDTH + h * HEAD_DIM) for h in heads]
    v = [conv_cols(2 * GDN_WIDTH + h * HEAD_DIM) for h in heads]
    qn = [x * (lax.rsqrt(jnp.sum(x * x, axis=-1, keepdims=True) + EPS) * (HEAD_DIM ** -0.5)) for x in q]
    kn = [x * lax.rsqrt(jnp.sum(x * x, axis=-1, keepdims=True) + EPS) for x in k]
    bcol = [beta[:, SUBLANES + h:SUBLANES + h + 1] for h in heads]
    egcol = [eg[:, h:h + 1] for h in heads]
    dec = [jnp.exp(jnp.where(incl, gc[:, h:h + 1] - gct[h:h + 1, :], -jnp.inf)) for h in heads]
    kb = [kn[h] * bcol[h] for h in heads]
    kn16 = [x.astype(BF16) for x in kn]
    kq = [_dot_nt(jnp.concatenate([kb[h].astype(BF16), qn[h].astype(BF16)], axis=0), kn16[h]) for h in heads]
    a = [jnp.where(strict, kq[h][:r] * dec[h], 0.0) for h in heads]
    qk16 = [(kq[h][r:] * dec[h]).astype(BF16) for h in heads]
    tinv = _unit_lower_inverse(a, same_block, eye)
    sol = [_dot(tinv[h], jnp.concatenate([v[h] * bcol[h], kb[h] * egcol[h]], axis=1)) for h in heads]
    qg = [qn[h] * egcol[h] for h in heads]
    kdt16 = [(kn[h] * egl[:, h:h + 1]).T.astype(BF16) for h in heads]

    ws, qs = [], []
    for h in heads:
        w = sol[h][:, HEAD_DIM:]
        ws_parts, qs_parts = [], []
        for sgi in range(nb):
            lo = sgi * c
            wq = jnp.concatenate([w[lo:lo + c], qg[h][lo:lo + c]], axis=0)
            res = _dot(wq, s_ref[sgi, h])
            ws_parts.append(res[:c])
            qs_parts.append(res[c:])
        ws.append(ws_parts[0] if nb == 1 else jnp.concatenate(ws_parts, axis=0))
        qs.append(qs_parts[0] if nb == 1 else jnp.concatenate(qs_parts, axis=0))
    v16 = [(sol[h][:, :HEAD_DIM] - ws[h]).astype(BF16) for h in heads]
    o = [qs[h] + jnp.dot(qk16[h], v16[h], preferred_element_type=F32) for h in heads]
    for h in heads:
        for sgi in range(nb):
            if nb == 1:
                kdm = kdt16[h]
            else:
                kdm = jnp.where(colseg == sgi, kdt16[h], jnp.zeros_like(kdt16[h]))
            s_ref[sgi, h] = (s_ref[sgi, h] * gl_rows[sgi][:, h:h + 1]
                             + jnp.dot(kdm, v16[h], preferred_element_type=F32))
    for h in heads:
        on = o[h] * lax.rsqrt(jnp.mean(o[h] * o[h], axis=-1, keepdims=True) + EPS) * nw
        zh = z_ref[:, :, h * HEAD_DIM:(h + 1) * HEAD_DIM].reshape(r, HEAD_DIM)
        out = on * _silu(zh)
        o_ref[:, :, h * HEAD_DIM:(h + 1) * HEAD_DIM] = out.reshape(nb, c, HEAD_DIM).astype(o_ref.dtype)

    ext_ref[:, 0:SUBLANES, :] = ext_ref[:, c:c + SUBLANES, :]

    @pl.when(t == pl.num_programs(1) - 1)
    def _():
        sout_ref[...] = s_ref[...]


def _gdn(proj, conv_state8, s0, conv_w8, head_params, norm_w, nb, c):
    b, l, _ = proj.shape
    nt = l // c
    zb = COL_Z // GDN_WIDTH
    abb = COL_AB // LANES
    return pl.pallas_call(
        _gdn_kernel,
        grid=(b // nb, nt),
        in_specs=[pl.BlockSpec((nb, c, QKV_D_WIDTH), lambda i, t: (i, t, 0)),
                  pl.BlockSpec((nb, c, GDN_WIDTH), lambda i, t: (i, t, zb)),
                  pl.BlockSpec((nb, c, LANES), lambda i, t: (i, t, abb)),
                  pl.BlockSpec((nb, SUBLANES, QKV_D_WIDTH), lambda i, t: (i, 0, 0)),
                  pl.BlockSpec((nb, GDN_HEADS, HEAD_DIM, HEAD_DIM), lambda i, t: (i, 0, 0, 0)),
                  pl.BlockSpec((SUBLANES, QKV_D_WIDTH), lambda i, t: (0, 0)),
                  pl.BlockSpec((SUBLANES, LANES), lambda i, t: (0, 0)),
                  pl.BlockSpec((1, HEAD_DIM), lambda i, t: (0, 0))],
        out_specs=[pl.BlockSpec((nb, c, GDN_WIDTH), lambda i, t: (i, t, 0)),
                   pl.BlockSpec((nb, GDN_HEADS, HEAD_DIM, HEAD_DIM), lambda i, t: (i, 0, 0, 0))],
        out_shape=[jax.ShapeDtypeStruct((b, l, GDN_WIDTH), BF16),
                   jax.ShapeDtypeStruct((b, GDN_HEADS, HEAD_DIM, HEAD_DIM), F32)],
        scratch_shapes=[pltpu.VMEM((nb, SUBLANES + c, QKV_D_WIDTH), F32),
                        pltpu.VMEM((nb, GDN_HEADS, HEAD_DIM, HEAD_DIM), F32)],
        compiler_params=_params("arbitrary", "arbitrary"),
        name="gated_delta_rule",
    )(proj, proj, proj, conv_state8, s0, conv_w8, head_params, norm_w.reshape(1, HEAD_DIM))


def _head_rms(x, w):
    return x * lax.rsqrt(jnp.mean(x * x, axis=-1, keepdims=True) + EPS) * w


def _swa_kernel(sink_ref, q_ref, k_ref, v_ref, kp_ref, vp_ref, qw_ref, kw_ref, o_ref, kn_ref,
                *, chunk, pos0, norm_prev):
    tq = q_ref.shape[1]
    tk = max(tq, LANES)
    m = SWA_GROUP * tq
    t0 = pos0 + pl.program_id(1) * tq
    qw = qw_ref[...]
    kw = kw_ref[...]

    rown = lax.broadcasted_iota(jnp.int32, (m, 1), 0)
    grp = rown // tq
    qi = rown - grp * tq
    cs = (qi // chunk) * chunk
    kj = lax.broadcasted_iota(jnp.int32, (1, tk), 1)
    pj = lax.broadcasted_iota(jnp.int32, (1, WINDOW), 1)
    vis_own = jnp.logical_and(kj < cs + chunk, kj < tq)
    vis_prev = jnp.logical_and(pj - WINDOW >= cs - WINDOW, t0 - WINDOW + pj >= 0)
    dist_own = jnp.abs(qi - kj).astype(F32)
    dist_prev = (qi + WINDOW - pj).astype(F32)
    scale = HEAD_DIM ** -0.5

    for kvh in range(SWA_KV_HEADS):
        slope = jnp.zeros((m, 1), F32)
        sink = jnp.zeros((m, 1), F32)
        for gi in range(SWA_GROUP):
            hq = kvh * SWA_GROUP + gi
            slope = jnp.where(grp == gi, 2.0 ** (-8.0 * (hq + 1) / SWA_HEADS), slope)
            sink = jnp.where(grp == gi, sink_ref[hq], sink)
        qs = jnp.concatenate(
            [_head_rms(q_ref[0, :, (kvh * SWA_GROUP + gi) * HEAD_DIM:(kvh * SWA_GROUP + gi + 1) * HEAD_DIM], qw)
             for gi in range(SWA_GROUP)], axis=0)
        ksl = slice(kvh * HEAD_DIM, (kvh + 1) * HEAD_DIM)
        k_own = _head_rms(k_ref[0, :, ksl], kw)
        kn_ref[0, :, ksl] = k_own
        v_own = v_ref[0, :, ksl]
        if tk > tq:
            pad = jnp.zeros((tk - tq, HEAD_DIM), F32)
            k_own = jnp.concatenate([k_own, pad], axis=0)
            v_own = jnp.concatenate([v_own, pad], axis=0)
        k_prev = kp_ref[0, :, ksl]
        if norm_prev:
            k_prev = _head_rms(k_prev, kw)
        v_prev = vp_ref[0, :, ksl]

        s_own = jnp.where(vis_own, _dot_nt(qs, k_own) * scale - slope * dist_own, -jnp.inf)
        s_prev = jnp.where(vis_prev, _dot_nt(qs, k_prev) * scale - slope * dist_prev, -jnp.inf)
        mx = jnp.maximum(jnp.maximum(jnp.max(s_own, axis=-1, keepdims=True),
                                     jnp.max(s_prev, axis=-1, keepdims=True)), sink)
        p_own = jnp.exp(s_own - mx)
        p_prev = jnp.exp(s_prev - mx)
        den = (jnp.sum(p_own, axis=-1, keepdims=True) + jnp.sum(p_prev, axis=-1, keepdims=True)
               + jnp.exp(sink - mx))
        inv = 1.0 / den
        o = _dot(p_own * inv, v_own) + _dot(p_prev * inv, v_prev)
        for gi in range(SWA_GROUP):
            hq = kvh * SWA_GROUP + gi
            o_ref[0, :, hq * HEAD_DIM:(hq + 1) * HEAD_DIM] = o[gi * tq:(gi + 1) * tq].astype(o_ref.dtype)


def _swa(proj, k_prev_src, v_prev_src, prev_from_proj, q_norm_w, k_norm_w, sinks, tq, chunk, pos0):
    b, l, _ = proj.shape
    qb = COL_QA // SWA_WIDTH
    kb = COL_KA // KV_WIDTH
    vb = COL_VA // KV_WIDTH
    if prev_from_proj:
        assert tq == WINDOW
        kp_spec = pl.BlockSpec((1, WINDOW, KV_WIDTH), lambda i, t: (i, jnp.maximum(t - 1, 0), kb))
        vp_spec = pl.BlockSpec((1, WINDOW, KV_WIDTH), lambda i, t: (i, jnp.maximum(t - 1, 0), vb))
    else:
        assert l == tq
        kp_spec = pl.BlockSpec((1, WINDOW, KV_WIDTH), lambda i, t: (i, 0, 0))
        vp_spec = pl.BlockSpec((1, WINDOW, KV_WIDTH), lambda i, t: (i, 0, 0))
    kernel = functools.partial(_swa_kernel, chunk=chunk, pos0=pos0, norm_prev=prev_from_proj)
    return pl.pallas_call(
        kernel,
        grid=(b, l // tq),
        in_specs=[pl.BlockSpec(memory_space=pltpu.SMEM),
                  pl.BlockSpec((1, tq, SWA_WIDTH), lambda i, t: (i, t, qb)),
                  pl.BlockSpec((1, tq, KV_WIDTH), lambda i, t: (i, t, kb)),
                  pl.BlockSpec((1, tq, KV_WIDTH), lambda i, t: (i, t, vb)),
                  kp_spec, vp_spec,
                  pl.BlockSpec((1, HEAD_DIM), lambda i, t: (0, 0)),
                  pl.BlockSpec((1, HEAD_DIM), lambda i, t: (0, 0))],
        out_specs=[pl.BlockSpec((1, tq, SWA_WIDTH), lambda i, t: (i, t, 0)),
                   pl.BlockSpec((1, tq, KV_WIDTH), lambda i, t: (i, t, 0))],
        out_shape=[jax.ShapeDtypeStruct((b, l, SWA_WIDTH), BF16),
                   jax.ShapeDtypeStruct((b, l, KV_WIDTH), F32)],
        compiler_params=_params("arbitrary", "arbitrary"),
        name="sliding_window_attention",
    )(sinks, proj, proj, proj, k_prev_src, v_prev_src,
      q_norm_w.reshape(1, HEAD_DIM), k_norm_w.reshape(1, HEAD_DIM))


def _outproj_kernel(x_ref, od_ref, oa_ref, mod_ref, w_ref, o_ref):
    tb, tl, d = x_ref.shape
    od = od_ref[...].reshape(tb * tl, GDN_WIDTH)
    oa = oa_ref[...].reshape(tb * tl, SWA_WIDTH)
    acc = (jnp.dot(od, w_ref[0:GDN_WIDTH, :], preferred_element_type=F32)
           + jnp.dot(oa, w_ref[GDN_WIDTH:GDN_WIDTH + SWA_WIDTH, :], preferred_element_type=F32))
    o_ref[...] = x_ref[...] + mod_ref[:, 2:3, :] * acc.reshape(tb, tl, d)


def _outproj(x, o_d, o_a, mod, w_o, tb, tl):
    b, l, d = x.shape
    return pl.pallas_call(
        _outproj_kernel,
        grid=(b // tb, l // tl),
        in_specs=[pl.BlockSpec((tb, tl, d), lambda i, t: (i, t, 0)),
                  pl.BlockSpec((tb, tl, GDN_WIDTH), lambda i, t: (i, t, 0)),
                  pl.BlockSpec((tb, tl, SWA_WIDTH), lambda i, t: (i, t, 0)),
                  pl.BlockSpec((tb, N_MOD, d), lambda i, t: (i, 0, 0)),
                  pl.BlockSpec(w_o.shape, lambda i, t: (0, 0))],
        out_specs=pl.BlockSpec((tb, tl, d), lambda i, t: (i, t, 0)),
        out_shape=jax.ShapeDtypeStruct((b, l, d), F32),
        compiler_params=_params("arbitrary", "arbitrary"),
        name="out_projection",
    )(x, o_d, o_a, mod, w_o)


def _ffn_kernel(x_ref, mod_ref, nw_ref, wg_ref, wu_ref, cwg_ref, cwu_ref, bg_ref, bu_ref,
                stg_ref, stu_ref, wd_ref, y_ref, lastg_ref, lastu_ref,
                h_ref, act_ref, extg_ref, extu_ref, carg_ref, caru_ref):
    tb, tl, d = x_ref.shape
    tf = wg_ref.shape[1]
    t = pl.program_id(1)
    j = pl.program_id(2)

    halves = ((wg_ref, cwg_ref, bg_ref, stg_ref, extg_ref, carg_ref, lastg_ref),
              (wu_ref, cwu_ref, bu_ref, stu_ref, extu_ref, caru_ref, lastu_ref))

    @pl.when(j == 0)
    def _():
        h = _modulated_norm(x_ref[...], nw_ref[...], mod_ref[:, 3:4, :], mod_ref[:, 4:5, :])
        h_ref[...] = h.reshape(tb * tl, d).astype(BF16)
        y_ref[...] = jnp.zeros_like(y_ref)

    @pl.when(t == 0)
    def _():
        for _, _, _, st_ref, ext_ref, _, _ in halves:
            ext_ref[:, 0:SUBLANES, :] = st_ref[...]

    @pl.when(t > 0)
    def _():
        for _, _, _, _, ext_ref, car_ref, _ in halves:
            ext_ref[:, 0:SUBLANES, :] = car_ref[j]

    h = h_ref[...]
    ncb = tf // FFN_COLS
    for cb in range(ncb):
        cs = slice(cb * FFN_COLS, (cb + 1) * FFN_COLS)
        for w_ref, _, _, _, ext_ref, _, _ in halves:
            u = jnp.dot(h, w_ref[:, cs], preferred_element_type=F32)
            ext_ref[:, SUBLANES:SUBLANES + tl, cs] = u.reshape(tb, tl, FFN_COLS)
    for _, _, _, _, ext_ref, car_ref, last_ref in halves:
        tail = ext_ref[:, tl:tl + SUBLANES, :]
        car_ref[j] = tail
        last_ref[:, 0] = tail

    if tb == 1:
        chunks = [(slice(0, 1), r0, FFN_ROWS) for r0 in range(0, tl, FFN_ROWS)]
    else:
        nbc = max(1, FFN_ROWS // tl)
        chunks = [(slice(b0, b0 + nbc), 0, tl) for b0 in range(0, tb, nbc)]
    for cb in range(ncb):
        cs = slice(cb * FFN_COLS, (cb + 1) * FFN_COLS)
        for bs, r0, nr in chunks:
            conv = []
            for _, cw_ref, b_ref, _, ext_ref, _, _ in halves:
                acc = b_ref[:, cs]
                for i in range(FFN_CONV):
                    lo = SUBLANES - (FFN_CONV - 1) + i + r0
                    acc = acc + cw_ref[i:i + 1, cs] * ext_ref[bs, lo:lo + nr, cs]
                conv.append(acc)
            act = _silu(conv[0]) * conv[1]
            nbs = bs.stop - bs.start
            row0 = bs.start * tl + r0
            act_ref[row0:row0 + nbs * nr, cs] = act.reshape(nbs * nr, FFN_COLS).astype(BF16)

    y_ref[...] += jnp.dot(act_ref[...], wd_ref[...], preferred_element_type=F32).reshape(tb, tl, d)

    @pl.when(j == pl.num_programs(2) - 1)
    def _():
        y_ref[...] = x_ref[...] + mod_ref[:, 5:6, :] * y_ref[...]


def _ffn(x, mod, norm_w, w_up, conv_w8, conv_b, state8, w_down, tb, tl, tf):
    b, l, d = x.shape
    d_ff = w_down.shape[0]
    nj = d_ff // tf
    cb = conv_b.reshape(1, 2 * d_ff)
    lo = lambda i, t, j: (0, j)
    hi = lambda i, t, j: (0, nj + j)
    return pl.pallas_call(
        _ffn_kernel,
        grid=(b // tb, l // tl, nj),
        in_specs=[pl.BlockSpec((tb, tl, d), lambda i, t, j: (i, t, 0)),
                  pl.BlockSpec((tb, N_MOD, d), lambda i, t, j: (i, 0, 0)),
                  pl.BlockSpec((1, d), lambda i, t, j: (0, 0)),
                  pl.BlockSpec((d, tf), lo), pl.BlockSpec((d, tf), hi),
                  pl.BlockSpec((SUBLANES, tf), lo), pl.BlockSpec((SUBLANES, tf), hi),
                  pl.BlockSpec((1, tf), lo), pl.BlockSpec((1, tf), hi),
                  pl.BlockSpec((tb, SUBLANES, tf), lambda i, t, j: (i, 0, j)),
                  pl.BlockSpec((tb, SUBLANES, tf), lambda i, t, j: (i, 0, nj + j)),
                  pl.BlockSpec((tf, d), lambda i, t, j: (j, 0))],
        out_specs=[pl.BlockSpec((tb, tl, d), lambda i, t, j: (i, t, 0)),
                   pl.BlockSpec((tb, 1, SUBLANES, tf), lambda i, t, j: (i, t, 0, j)),
                   pl.BlockSpec((tb, 1, SUBLANES, tf), lambda i, t, j: (i, t, 0, j))],
        out_shape=[jax.ShapeDtypeStruct((b, l, d), F32),
                   jax.ShapeDtypeStruct((b, l // tl, SUBLANES, d_ff), F32),
                   jax.ShapeDtypeStruct((b, l // tl, SUBLANES, d_ff), F32)],
        scratch_shapes=[pltpu.VMEM((tb * tl, d), BF16),
                        pltpu.VMEM((tb * tl, tf), BF16),
                        pltpu.VMEM((tb, SUBLANES + tl, tf), F32),
                        pltpu.VMEM((tb, SUBLANES + tl, tf), F32),
                        pltpu.VMEM((nj, tb, SUBLANES, tf), F32),
                        pltpu.VMEM((nj, tb, SUBLANES, tf), F32)],
        compiler_params=_params("arbitrary", "arbitrary", "arbitrary"),
        name="conv_ffn",
    )(x, mod, norm_w.reshape(1, d), w_up, w_up, conv_w8, conv_w8, cb, cb, state8, state8, w_down)


def _pad_rows_front(x, rows):
    pad = rows - x.shape[1]
    return jnp.pad(x, ((0, 0), (pad, 0), (0, 0)))


def _layer(x, mod, conv_prev, s0, k_prev, v_prev, ffn_prev, pos0, wts, cfg):
    b, l, d = x.shape
    proj = _inproj(x, mod, wts["norm1_w"], wts["w_in"], cfg["tb"], cfg["tl_in"], cfg["tn_in"])

    nb = cfg["gdn_nb"]
    c = GDN_ROWS // nb
    o_d, s_new = _gdn(proj, _pad_rows_front(conv_prev, SUBLANES), s0, wts["conv_qkv_w8"],
                      wts["head_params"], wts["gdn_norm_w"], nb, c)

    chunk = min(CHUNK, l)
    if k_prev is None:
        o_a, kn = _swa(proj, proj, proj, True, wts["q_norm_w"], wts["k_norm_w"], wts["sinks"],
                       cfg["tq"], chunk, pos0)
    else:
        o_a, kn = _swa(proj, k_prev.reshape(b, WINDOW, KV_WIDTH), v_prev.reshape(b, WINDOW, KV_WIDTH), False,
                       wts["q_norm_w"], wts["k_norm_w"], wts["sinks"], cfg["tq"], chunk, pos0)

    x1 = _outproj(x, o_d, o_a, mod, wts["w_o"], cfg["tb"], cfg["tl_out"])
    y, last_g, last_u = _ffn(x1, mod, wts["norm2_w"], wts["w_up"], wts["ffn_conv_w8"], wts["ffn_conv_b"],
                             _pad_rows_front(ffn_prev, SUBLANES), wts["w_down"],
                             cfg["tb"], cfg["tl_ffn"], cfg["tf"])

    qkv_new = proj[:, l - (GDN_CONV - 1):, COL_QKV:COL_QKV + QKV_D_WIDTH]
    if l >= GDN_CONV - 1:
        conv_new = qkv_new
    else:
        conv_new = jnp.concatenate([conv_prev, qkv_new], axis=1)[:, -(GDN_CONV - 1):]
    if l >= WINDOW:
        k_cache = kn[:, l - WINDOW:]
        v_cache = proj[:, l - WINDOW:, COL_VA:COL_VA + KV_WIDTH]
    else:
        v_new = proj[:, :, COL_VA:COL_VA + KV_WIDTH]
        k_cache = jnp.concatenate([k_prev.reshape(b, WINDOW, KV_WIDTH)[:, l:], kn], axis=1)
        v_cache = jnp.concatenate([v_prev.reshape(b, WINDOW, KV_WIDTH)[:, l:], v_new], axis=1)
    k_cache = k_cache.reshape(b, WINDOW, SWA_KV_HEADS, HEAD_DIM)
    v_cache = v_cache.reshape(b, WINDOW, SWA_KV_HEADS, HEAD_DIM)
    ffn_new = jnp.concatenate([last_g[:, -1, SUBLANES - (FFN_CONV - 1):],
                               last_u[:, -1, SUBLANES - (FFN_CONV - 1):]], axis=-1)
    return y, conv_new, s_new, k_cache, v_cache, ffn_new


def _prep_weights(l, ada_w, ada_b, norm1_w, norm2_w, w_in, conv_qkv_w, a_log, dt_bias, gdn_norm_w,
                  q_norm_w, k_norm_w, sinks, w_o, w_up, ffn_conv_w, ffn_conv_b, w_down):
    d = w_in.shape[1]
    ab0 = QKV_D_WIDTH + GDN_WIDTH
    ab1 = ab0 + 2 * GDN_HEADS
    w = w_in[l]
    w_in_r = jnp.concatenate(
        [w[:, :ab0], w[:, ab1:], w[:, ab0:ab1], jnp.zeros((d, LANES - 2 * GDN_HEADS), w.dtype)], axis=1).astype(BF16)
    assert w_in_r.shape[1] == PROJ_WIDTH
    hp = jnp.zeros((SUBLANES, LANES), F32)
    hp = hp.at[0, :GDN_HEADS].set(a_log[l].astype(F32)).at[1, :GDN_HEADS].set(dt_bias[l].astype(F32))
    return {
        "ada_w": ada_w[l], "ada_b": ada_b[l], "norm1_w": norm1_w[l], "norm2_w": norm2_w[l],
        "w_in": w_in_r,
        "conv_qkv_w8": jnp.pad(conv_qkv_w[l], ((0, SUBLANES - GDN_CONV), (0, 0))),
        "head_params": hp,
        "gdn_norm_w": gdn_norm_w[l], "q_norm_w": q_norm_w[l], "k_norm_w": k_norm_w[l], "sinks": sinks[l],
        "w_o": w_o[l].astype(BF16), "w_up": w_up[l].astype(BF16),
        "ffn_conv_w8": jnp.pad(ffn_conv_w[l], ((0, SUBLANES - FFN_CONV), (0, 0))),
        "ffn_conv_b": ffn_conv_b[l], "w_down": w_down[l].astype(BF16),
    }


PROMPT_CFG = dict(tb=1, tl_in=1024, tn_in=1152, gdn_nb=1, tq=128, tl_out=512, tl_ffn=512, tf=512)


def _sample_cfg(b, l):
    return dict(tb=b, tl_in=l, tn_in=1152, gdn_nb=GDN_ROWS // l, tq=l, tl_out=l, tl_ffn=l, tf=512)


def kernel(x_prompt, x_sample, state_conv_qkv, state_delta, cache_swa_k, cache_swa_v, state_ffn_conv,
           c_prompt, c_sample, ada_w, ada_b, norm1_w, norm2_w, w_in, conv_qkv_w, a_log, dt_bias,
           gdn_norm_w, q_norm_w, k_norm_w, sinks, w_o, w_up, ffn_conv_w, ffn_conv_b, w_down):
    depth = w_in.shape[0]
    bp, lp, d = x_prompt.shape
    bs, ls, _ = x_sample.shape
    d_ff = w_down.shape[1]
    xp, xs = x_prompt, x_sample
    acc_p = [[] for _ in range(5)]
    acc_s = [[] for _ in range(5)]
    c_all = jnp.concatenate([c_prompt, c_sample], axis=0)
    c_rows = -(-c_all.shape[0] // SUBLANES) * SUBLANES
    c_all = jnp.pad(c_all, ((0, c_rows - c_all.shape[0]), (0, 0)))
    for l in range(depth):
        wts = _prep_weights(l, ada_w, ada_b, norm1_w, norm2_w, w_in, conv_qkv_w, a_log, dt_bias, gdn_norm_w,
                            q_norm_w, k_norm_w, sinks, w_o, w_up, ffn_conv_w, ffn_conv_b, w_down)
        mod = _modulation(c_all, wts["ada_w"], wts["ada_b"]).reshape(c_rows, N_MOD, d)
        mod_p, mod_s = mod[:bp], mod[bp:bp + bs]
        dt = xp.dtype
        zero_state = (jnp.zeros((bp, GDN_CONV - 1, QKV_D_WIDTH), dt),
                      jnp.zeros((bp, GDN_HEADS, HEAD_DIM, HEAD_DIM), dt),
                      None, None,
                      jnp.zeros((bp, FFN_CONV - 1, 2 * d_ff), dt))
        xp, *new_p = _layer(xp, mod_p, *zero_state, 0, wts, PROMPT_CFG)
        xs, *new_s = _layer(xs, mod_s, state_conv_qkv[l], state_delta[l], cache_swa_k[l], cache_swa_v[l],
                            state_ffn_conv[l], PAST_LEN, wts, _sample_cfg(bs, ls))
        for acc, t in zip(acc_p, new_p):
            acc.append(t)
        for acc, t in zip(acc_s, new_s):
            acc.append(t)
    outs_p = [jnp.stack(a) for a in acc_p]
    outs_s = [jnp.stack(a) for a in acc_s]
    return (xp, xs, *outs_p, *outs_s)
```

```python
import functools
import math

import jax
import jax.numpy as jnp
from jax import lax
from jax.experimental import pallas as pl
from jax.experimental.pallas import tpu as pltpu

F32 = jnp.float32
BF16 = jnp.bfloat16

HEAD_DIM = 128
GDN_HEADS = 8
GDN_CONV = 4
SWA_HEADS = 8
SWA_KV_HEADS = 2
SWA_GROUP = SWA_HEADS // SWA_KV_HEADS
WINDOW = 128
FFN_CONV = 3
N_MOD = 6
EPS = 1e-6
CHUNK = 64
PAST_LEN = 4096

GDN_WIDTH = GDN_HEADS * HEAD_DIM
SWA_WIDTH = SWA_HEADS * HEAD_DIM
KV_WIDTH = SWA_KV_HEADS * HEAD_DIM
QKV_D_WIDTH = 3 * GDN_WIDTH

LANES = 128
SUBLANES = 8
GDN_ROWS = 128
INV_BASE = 16
VMEM_LIMIT = 56 * 1024 * 1024
FFN_COLS = 256
FFN_ROWS = 32
FFN_TF = 512
INPROJ_TN = 1152

COL_QKV = 0
COL_Z = QKV_D_WIDTH
COL_QA = COL_Z + GDN_WIDTH
COL_KA = COL_QA + SWA_WIDTH
COL_VA = COL_KA + KV_WIDTH
COL_AB = COL_VA + KV_WIDTH
PROJ_WIDTH = COL_AB + LANES


def _dot(a, b):
    return jnp.dot(a.astype(BF16), b.astype(BF16), preferred_element_type=F32)


def _dot_nt(a, b):
    return lax.dot_general(a.astype(BF16), b.astype(BF16), (((1,), (1,)), ((), ())),
                           preferred_element_type=F32)


def _silu(x):
    return x * jax.nn.sigmoid(x)


def _softplus(x):
    return jnp.maximum(x, 0.0) + jnp.log1p(jnp.exp(-jnp.abs(x)))


def _params(*sem):
    return pltpu.CompilerParams(dimension_semantics=sem, vmem_limit_bytes=VMEM_LIMIT)


def _mod_kernel(c_ref, w_ref, b_ref, o_ref):
    o_ref[...] = _dot(_silu(c_ref[...]), w_ref[...]) + b_ref[...]


def _modulation(c, ada_w, ada_b, tn=1024):
    rows, d = c.shape
    n = ada_w.shape[1]
    return pl.pallas_call(
        _mod_kernel,
        grid=(n // tn,),
        in_specs=[pl.BlockSpec((rows, d), lambda j: (0, 0)),
                  pl.BlockSpec((d, tn), lambda j: (0, j)),
                  pl.BlockSpec((1, tn), lambda j: (0, j))],
        out_specs=pl.BlockSpec((rows, tn), lambda j: (0, j)),
        out_shape=jax.ShapeDtypeStruct((rows, n), F32),
        compiler_params=_params("arbitrary"),
        name="modulation",
    )(c, ada_w, ada_b.reshape(1, n))


def _modulated_norm(x, nw, shift, scale):
    y = x * lax.rsqrt(jnp.mean(x * x, axis=-1, keepdims=True) + EPS) * nw
    return y * (1.0 + scale) + shift


def _inproj_kernel(x_ref, mod_ref, nw_ref, w_ref, o_ref, h_ref):
    tb, tl, d = x_ref.shape

    @pl.when(pl.program_id(2) == 0)
    def _():
        h = _modulated_norm(x_ref[...], nw_ref[...], mod_ref[:, 0:1, :], mod_ref[:, 1:2, :])
        h_ref[...] = h.reshape(tb * tl, d).astype(BF16)

    o_ref[...] = jnp.dot(h_ref[...], w_ref[...], preferred_element_type=F32).reshape(o_ref.shape)


def _inproj(x, mod, norm_w, w, tb, tl):
    b, l, d = x.shape
    nj, _, tn = w.shape
    n = nj * tn
    return pl.pallas_call(
        _inproj_kernel,
        grid=(b // tb, l // tl, nj),
        in_specs=[pl.BlockSpec((tb, tl, d), lambda i, t, j: (i, t, 0)),
                  pl.BlockSpec((tb, N_MOD, d), lambda i, t, j: (i, 0, 0)),
                  pl.BlockSpec((1, d), lambda i, t, j: (0, 0)),
                  pl.BlockSpec((None, d, tn), lambda i, t, j: (j, 0, 0))],
        out_specs=pl.BlockSpec((tb, tl, tn), lambda i, t, j: (i, t, j)),
        out_shape=jax.ShapeDtypeStruct((b, l, n), F32),
        scratch_shapes=[pltpu.VMEM((tb * tl, d), BF16)],
        compiler_params=_params("arbitrary", "arbitrary", "arbitrary"),
        name="in_projection",
    )(x, mod, norm_w.reshape(1, d), w)


def _unit_lower_inverse(a_list, same_block, eye):
    sizes = sorted(same_block)
    p = [-jnp.where(same_block[sizes[0]], a, 0.0) for a in a_list]
    x = [eye + pi for pi in p]
    k = 2
    while k < sizes[0]:
        p = [_dot(pi, pi) for pi in p]
        x = [xi + _dot(xi, pi) for xi, pi in zip(x, p)]
        k *= 2
    for s_prev, s in zip(sizes[:-1], sizes[1:]):
        off = jnp.logical_and(same_block[s], jnp.logical_not(same_block[s_prev]))
        xa = [_dot(xi, jnp.where(off, a, 0.0)) for xi, a in zip(x, a_list)]
        x = [xi - _dot(xai, xi) for xi, xai in zip(x, xa)]
    return x


def _gdn_kernel(qkv_ref, z_ref, ab_ref, cst_ref, s0_ref, cw_ref, hp_ref, nw_ref,
                o_ref, sout_ref, ext_ref, s_ref):
    nb, c, _ = qkv_ref.shape
    r = nb * c
    t = pl.program_id(1)

    @pl.when(t == 0)
    def _():
        ext_ref[:, 0:SUBLANES, :] = cst_ref[...]
        s_ref[...] = s0_ref[...]

    ext_ref[:, SUBLANES:SUBLANES + c, :] = qkv_ref[...]

    ab = ab_ref[...].reshape(r, LANES)
    hp = hp_ref[...]
    g = -jnp.exp(hp[0:1, :]) * _softplus(ab + hp[1:2, :])
    beta = jax.nn.sigmoid(ab)
    pos = jnp.bitwise_and(lax.broadcasted_iota(jnp.int32, (r, LANES), 0), c - 1)
    gc = g
    s = 1
    while s < c:
        gc = gc + jnp.where(pos >= s, pltpu.roll(gc, s, 0), 0.0)
        s *= 2
    if nb == 1:
        gcl = jnp.broadcast_to(gc[r - 1:r, :], (r, LANES))
    else:
        gcl = jnp.where(pos == c - 1, gc, 0.0)
        s = 1
        while s < c:
            gcl = gcl + jnp.where(pos < c - s, pltpu.roll(gcl, r - s, 0), 0.0)
            s *= 2
    eg = jnp.exp(gc)
    egl = jnp.exp(gcl - gc)
    gl_rows = [jnp.exp(gc[sgi * c + c - 1:sgi * c + c, :]) for sgi in range(nb)]
    gct = gc.T

    row = lax.broadcasted_iota(jnp.int32, (r, r), 0)
    col = lax.broadcasted_iota(jnp.int32, (r, r), 1)

    def same(sz):
        sh = int(math.log2(sz))
        return jnp.right_shift(row, sh) == jnp.right_shift(col, sh)

    sizes = []
    sz = min(INV_BASE, c)
    while sz <= c:
        sizes.append(sz)
        sz *= 2
    same_block = {sz: same(sz) for sz in sizes}
    seg = same_block[c]
    incl = jnp.logical_and(seg, row >= col)
    strict = jnp.logical_and(seg, row > col)
    eye = (row == col).astype(F32)
    colseg = jnp.right_shift(lax.broadcasted_iota(jnp.int32, (HEAD_DIM, r), 1), int(math.log2(c)))

    cw = cw_ref[...]

    def conv_cols(c0):
        cols = slice(c0, c0 + HEAD_DIM)
        if nb == 1:
            full = ext_ref[0, :, cols]
            acc = cw[GDN_CONV - 1:GDN_CONV, cols] * full[SUBLANES:]
            for i in range(GDN_CONV - 1):
                acc = acc + cw[i:i + 1, cols] * pltpu.roll(full, GDN_CONV - 1 - i, 0)[SUBLANES:]
            return _silu(acc)
        acc = cw[0:1, cols] * ext_ref[:, SUBLANES - 3:SUBLANES - 3 + c, cols]
        for i in range(1, GDN_CONV):
            lo = SUBLANES - 3 + i
            acc = acc + cw[i:i + 1, cols] * ext_ref[:, lo:lo + c, cols]
        return _silu(acc).reshape(r, HEAD_DIM)

    nw = nw_ref[...]
    heads = range(GDN_HEADS)
    q = [conv_cols(h * HEAD_DIM) for h in heads]
    k = [conv_cols(GDN_WIDTH + h * HEAD_DIM) for h in heads]
    v = [conv_cols(2 * GDN_WIDTH + h * HEAD_DIM) for h in heads]
    qn = [x * (lax.rsqrt(jnp.sum(x * x, axis=-1, keepdims=True) + EPS) * (HEAD_DIM ** -0.5)) for x in q]
    kn = [x * lax.rsqrt(jnp.sum(x * x, axis=-1, keepdims=True) + EPS) for x in k]
    bcol = [beta[:, SUBLANES + h:SUBLANES + h + 1] for h in heads]
    egcol = [eg[:, h:h + 1] for h in heads]
    dec = [jnp.exp(jnp.where(incl, gc[:, h:h + 1] - gct[h:h + 1, :], -jnp.inf)) for h in heads]
    kb = [kn[h] * bcol[h] for h in heads]
    kn16 = [x.astype(BF16) for x in kn]
    kq = [_dot_nt(jnp.concatenate([kb[h].astype(BF16), qn[h].astype(BF16)], axis=0), kn16[h]) for h in heads]
    a = [jnp.where(strict, kq[h][:r] * dec[h], 0.0) for h in heads]
    qk16 = [(kq[h][r:] * dec[h]).astype(BF16) for h in heads]
    tinv = _unit_lower_inverse(a, same_block, eye)
    sol = [_dot(tinv[h], jnp.concatenate([v[h] * bcol[h], kb[h] * egcol[h]], axis=1)) for h in heads]
    qg = [qn[h] * egcol[h] for h in heads]
    kdt16 = [(kn[h] * egl[:, h:h + 1]).T.astype(BF16) for h in heads]

    ws, qs = [], []
    for h in heads:
        w = sol[h][:, HEAD_DIM:]
        ws_parts, qs_parts = [], []
        for sgi in range(nb):
            lo = sgi * c
            wq = jnp.concatenate([w[lo:lo + c], qg[h][lo:lo + c]], axis=0)
            res = _dot(wq, s_ref[sgi, h])
            ws_parts.append(res[:c])
            qs_parts.append(res[c:])
        ws.append(ws_parts[0] if nb == 1 else jnp.concatenate(ws_parts, axis=0))
        qs.append(qs_parts[0] if nb == 1 else jnp.concatenate(qs_parts, axis=0))
    v16 = [(sol[h][:, :HEAD_DIM] - ws[h]).astype(BF16) for h in heads]
    o = [qs[h] + jnp.dot(qk16[h], v16[h], preferred_element_type=F32) for h in heads]
    for h in heads:
        for sgi in range(nb):
            if nb == 1:
                kdm = kdt16[h]
            else:
                kdm = jnp.where(colseg == sgi, kdt16[h], jnp.zeros_like(kdt16[h]))
            s_ref[sgi, h] = (s_ref[sgi, h] * gl_rows[sgi][:, h:h + 1]
                             + jnp.dot(kdm, v16[h], preferred_element_type=F32))
    for h in heads:
        on = o[h] * lax.rsqrt(jnp.mean(o[h] * o[h], axis=-1, keepdims=True) + EPS) * nw
        zh = z_ref[:, :, h * HEAD_DIM:(h + 1) * HEAD_DIM].reshape(r, HEAD_DIM)
        out = on * _silu(zh)
        o_ref[:, :, h * HEAD_DIM:(h + 1) * HEAD_DIM] = out.reshape(nb, c, HEAD_DIM).astype(o_ref.dtype)

    ext_ref[:, 0:SUBLANES, :] = ext_ref[:, c:c + SUBLANES, :]

    @pl.when(t == pl.num_programs(1) - 1)
    def _():
        sout_ref[...] = s_ref[...]


def _gdn(proj, conv_state8, s0, conv_w8, head_params, norm_w, nb, c):
    b, l, _ = proj.shape
    nt = l // c
    zb = COL_Z // GDN_WIDTH
    abb = COL_AB // LANES
    return pl.pallas_call(
        _gdn_kernel,
        grid=(b // nb, nt),
        in_specs=[pl.BlockSpec((nb, c, QKV_D_WIDTH), lambda i, t: (i, t, 0)),
                  pl.BlockSpec((nb, c, GDN_WIDTH), lambda i, t: (i, t, zb)),
                  pl.BlockSpec((nb, c, LANES), lambda i, t: (i, t, abb)),
                  pl.BlockSpec((nb, SUBLANES, QKV_D_WIDTH), lambda i, t: (i, 0, 0)),
                  pl.BlockSpec((nb, GDN_HEADS, HEAD_DIM, HEAD_DIM), lambda i, t: (i, 0, 0, 0)),
                  pl.BlockSpec((SUBLANES, QKV_D_WIDTH), lambda i, t: (0, 0)),
                  pl.BlockSpec((SUBLANES, LANES), lambda i, t: (0, 0)),
                  pl.BlockSpec((1, HEAD_DIM), lambda i, t: (0, 0))],
        out_specs=[pl.BlockSpec((nb, c, GDN_WIDTH), lambda i, t: (i, t, 0)),
                   pl.BlockSpec((nb, GDN_HEADS, HEAD_DIM, HEAD_DIM), lambda i, t: (i, 0, 0, 0))],
        out_shape=[jax.ShapeDtypeStruct((b, l, GDN_WIDTH), BF16),
                   jax.ShapeDtypeStruct((b, GDN_HEADS, HEAD_DIM, HEAD_DIM), F32)],
        scratch_shapes=[pltpu.VMEM((nb, SUBLANES + c, QKV_D_WIDTH), F32),
                        pltpu.VMEM((nb, GDN_HEADS, HEAD_DIM, HEAD_DIM), F32)],
        compiler_params=_params("arbitrary", "arbitrary"),
        name="gated_delta_rule",
    )(proj, proj, proj, conv_state8, s0, conv_w8, head_params, norm_w.reshape(1, HEAD_DIM))


def _head_rms(x, w):
    return x * lax.rsqrt(jnp.mean(x * x, axis=-1, keepdims=True) + EPS) * w


def _swa_kernel(sink_ref, q_ref, k_ref, v_ref, kp_ref, vp_ref, qw_ref, kw_ref, o_ref, kn_ref,
                *, chunk, pos0, norm_prev):
    tq = q_ref.shape[1]
    tk = max(tq, LANES)
    m = SWA_GROUP * tq
    t0 = pos0 + pl.program_id(1) * tq
    qw = qw_ref[...]
    kw = kw_ref[...]

    rown = lax.broadcasted_iota(jnp.int32, (m, 1), 0)
    grp = rown // tq
    qi = rown - grp * tq
    cs = (qi // chunk) * chunk
    kj = lax.broadcasted_iota(jnp.int32, (1, tk), 1)
    pj = lax.broadcasted_iota(jnp.int32, (1, WINDOW), 1)
    vis_own = jnp.logical_and(kj < cs + chunk, kj < tq)
    vis_prev = jnp.logical_and(pj - WINDOW >= cs - WINDOW, t0 - WINDOW + pj >= 0)
    dist_own = jnp.abs(qi - kj).astype(F32)
    dist_prev = (qi + WINDOW - pj).astype(F32)
    scale = HEAD_DIM ** -0.5

    for kvh in range(SWA_KV_HEADS):
        slope = jnp.zeros((m, 1), F32)
        sink = jnp.zeros((m, 1), F32)
        for gi in range(SWA_GROUP):
            hq = kvh * SWA_GROUP + gi
            slope = jnp.where(grp == gi, 2.0 ** (-8.0 * (hq + 1) / SWA_HEADS), slope)
            sink = jnp.where(grp == gi, sink_ref[hq], sink)
        qs = jnp.concatenate(
            [_head_rms(q_ref[0, :, (kvh * SWA_GROUP + gi) * HEAD_DIM:(kvh * SWA_GROUP + gi + 1) * HEAD_DIM], qw)
             for gi in range(SWA_GROUP)], axis=0)
        ksl = slice(kvh * HEAD_DIM, (kvh + 1) * HEAD_DIM)
        k_own = _head_rms(k_ref[0, :, ksl], kw)
        kn_ref[0, :, ksl] = k_own
        v_own = v_ref[0, :, ksl]
        if tk > tq:
            pad = jnp.zeros((tk - tq, HEAD_DIM), F32)
            k_own = jnp.concatenate([k_own, pad], axis=0)
            v_own = jnp.concatenate([v_own, pad], axis=0)
        k_prev = kp_ref[0, :, ksl]
        if norm_prev:
            k_prev = _head_rms(k_prev, kw)
        v_prev = vp_ref[0, :, ksl]

        s_own = jnp.where(vis_own, _dot_nt(qs, k_own) * scale - slope * dist_own, -jnp.inf)
        s_prev = jnp.where(vis_prev, _dot_nt(qs, k_prev) * scale - slope * dist_prev, -jnp.inf)
        mx = jnp.maximum(jnp.maximum(jnp.max(s_own, axis=-1, keepdims=True),
                                     jnp.max(s_prev, axis=-1, keepdims=True)), sink)
        p_own = jnp.exp(s_own - mx)
        p_prev = jnp.exp(s_prev - mx)
        den = (jnp.sum(p_own, axis=-1, keepdims=True) + jnp.sum(p_prev, axis=-1, keepdims=True)
               + jnp.exp(sink - mx))
        inv = 1.0 / den
        o = _dot(p_own * inv, v_own) + _dot(p_prev * inv, v_prev)
        for gi in range(SWA_GROUP):
            hq = kvh * SWA_GROUP + gi
            o_ref[0, :, hq * HEAD_DIM:(hq + 1) * HEAD_DIM] = o[gi * tq:(gi + 1) * tq].astype(o_ref.dtype)


def _swa(proj, k_prev_src, v_prev_src, prev_from_proj, q_norm_w, k_norm_w, sinks, tq, chunk, pos0):
    b, l, _ = proj.shape
    qb = COL_QA // SWA_WIDTH
    kb = COL_KA // KV_WIDTH
    vb = COL_VA // KV_WIDTH
    if prev_from_proj:
        assert tq == WINDOW
        kp_spec = pl.BlockSpec((1, WINDOW, KV_WIDTH), lambda i, t: (i, jnp.maximum(t - 1, 0), kb))
        vp_spec = pl.BlockSpec((1, WINDOW, KV_WIDTH), lambda i, t: (i, jnp.maximum(t - 1, 0), vb))
    else:
        assert l == tq
        kp_spec = pl.BlockSpec((1, WINDOW, KV_WIDTH), lambda i, t: (i, 0, 0))
        vp_spec = pl.BlockSpec((1, WINDOW, KV_WIDTH), lambda i, t: (i, 0, 0))
    kernel = functools.partial(_swa_kernel, chunk=chunk, pos0=pos0, norm_prev=prev_from_proj)
    return pl.pallas_call(
        kernel,
        grid=(b, l // tq),
        in_specs=[pl.BlockSpec(memory_space=pltpu.SMEM),
                  pl.BlockSpec((1, tq, SWA_WIDTH), lambda i, t: (i, t, qb)),
                  pl.BlockSpec((1, tq, KV_WIDTH), lambda i, t: (i, t, kb)),
                  pl.BlockSpec((1, tq, KV_WIDTH), lambda i, t: (i, t, vb)),
                  kp_spec, vp_spec,
                  pl.BlockSpec((1, HEAD_DIM), lambda i, t: (0, 0)),
                  pl.BlockSpec((1, HEAD_DIM), lambda i, t: (0, 0))],
        out_specs=[pl.BlockSpec((1, tq, SWA_WIDTH), lambda i, t: (i, t, 0)),
                   pl.BlockSpec((1, tq, KV_WIDTH), lambda i, t: (i, t, 0))],
        out_shape=[jax.ShapeDtypeStruct((b, l, SWA_WIDTH), BF16),
                   jax.ShapeDtypeStruct((b, l, KV_WIDTH), F32)],
        compiler_params=_params("arbitrary", "arbitrary"),
        name="sliding_window_attention",
    )(sinks, proj, proj, proj, k_prev_src, v_prev_src,
      q_norm_w.reshape(1, HEAD_DIM), k_norm_w.reshape(1, HEAD_DIM))


def _outproj_kernel(x_ref, od_ref, oa_ref, mod_ref, w_ref, o_ref):
    tb, tl, d = x_ref.shape
    od = od_ref[...].reshape(tb * tl, GDN_WIDTH)
    oa = oa_ref[...].reshape(tb * tl, SWA_WIDTH)
    acc = (jnp.dot(od, w_ref[0:GDN_WIDTH, :], preferred_element_type=F32)
           + jnp.dot(oa, w_ref[GDN_WIDTH:GDN_WIDTH + SWA_WIDTH, :], preferred_element_type=F32))
    o_ref[...] = x_ref[...] + mod_ref[:, 2:3, :] * acc.reshape(tb, tl, d)


def _outproj(x, o_d, o_a, mod, w_o, tb, tl):
    b, l, d = x.shape
    return pl.pallas_call(
        _outproj_kernel,
        grid=(b // tb, l // tl),
        in_specs=[pl.BlockSpec((tb, tl, d), lambda i, t: (i, t, 0)),
                  pl.BlockSpec((tb, tl, GDN_WIDTH), lambda i, t: (i, t, 0)),
                  pl.BlockSpec((tb, tl, SWA_WIDTH), lambda i, t: (i, t, 0)),
                  pl.BlockSpec((tb, N_MOD, d), lambda i, t: (i, 0, 0)),
                  pl.BlockSpec(w_o.shape, lambda i, t: (0, 0))],
        out_specs=pl.BlockSpec((tb, tl, d), lambda i, t: (i, t, 0)),
        out_shape=jax.ShapeDtypeStruct((b, l, d), F32),
        compiler_params=_params("arbitrary", "arbitrary"),
        name="out_projection",
    )(x, o_d, o_a, mod, w_o)


def _ffn_kernel(x_ref, mod_ref, nw_ref, wg_ref, wu_ref, cwg_ref, cwu_ref, bg_ref, bu_ref,
                stg_ref, stu_ref, wd_ref, y_ref, lastg_ref, lastu_ref,
                h_ref, act_ref, extg_ref, extu_ref, carg_ref, caru_ref):
    tb, tl, d = x_ref.shape
    tf = wg_ref.shape[1]
    t = pl.program_id(1)
    j = pl.program_id(2)

    halves = ((wg_ref, cwg_ref, bg_ref, stg_ref, extg_ref, carg_ref, lastg_ref),
              (wu_ref, cwu_ref, bu_ref, stu_ref, extu_ref, caru_ref, lastu_ref))

    @pl.when(j == 0)
    def _():
        h = _modulated_norm(x_ref[...], nw_ref[...], mod_ref[:, 3:4, :], mod_ref[:, 4:5, :])
        h_ref[...] = h.reshape(tb * tl, d).astype(BF16)
        y_ref[...] = jnp.zeros_like(y_ref)

    @pl.when(t == 0)
    def _():
        for _, _, _, st_ref, ext_ref, _, _ in halves:
            ext_ref[:, 0:SUBLANES, :] = st_ref[...]

    @pl.when(t > 0)
    def _():
        for _, _, _, _, ext_ref, car_ref, _ in halves:
            ext_ref[:, 0:SUBLANES, :] = car_ref[j]

    h = h_ref[...]
    ncb = tf // FFN_COLS
    for cb in range(ncb):
        cs = slice(cb * FFN_COLS, (cb + 1) * FFN_COLS)
        for w_ref, _, _, _, ext_ref, _, _ in halves:
            u = jnp.dot(h, w_ref[:, cs], preferred_element_type=F32)
            ext_ref[:, SUBLANES:SUBLANES + tl, cs] = u.reshape(tb, tl, FFN_COLS)
    for _, _, _, _, ext_ref, car_ref, last_ref in halves:
        tail = ext_ref[:, tl:tl + SUBLANES, :]
        car_ref[j] = tail
        last_ref[:, 0] = tail

    if tb == 1:
        chunks = [(slice(0, 1), r0, FFN_ROWS) for r0 in range(0, tl, FFN_ROWS)]
    else:
        nbc = max(1, FFN_ROWS // tl)
        chunks = [(slice(b0, b0 + nbc), 0, tl) for b0 in range(0, tb, nbc)]
    for cb in range(ncb):
        cs = slice(cb * FFN_COLS, (cb + 1) * FFN_COLS)
        for bs, r0, nr in chunks:
            conv = []
            for _, cw_ref, b_ref, _, ext_ref, _, _ in halves:
                acc = b_ref[:, cs]
                for i in range(FFN_CONV):
                    lo = SUBLANES - (FFN_CONV - 1) + i + r0
                    acc = acc + cw_ref[i:i + 1, cs] * ext_ref[bs, lo:lo + nr, cs]
                conv.append(acc)
            act = _silu(conv[0]) * conv[1]
            nbs = bs.stop - bs.start
            row0 = bs.start * tl + r0
            act_ref[row0:row0 + nbs * nr, cs] = act.reshape(nbs * nr, FFN_COLS).astype(BF16)

    y_ref[...] += jnp.dot(act_ref[...], wd_ref[...], preferred_element_type=F32).reshape(tb, tl, d)

    @pl.when(j == pl.num_programs(2) - 1)
    def _():
        y_ref[...] = x_ref[...] + mod_ref[:, 5:6, :] * y_ref[...]


def _ffn(x, mod, norm_w, w_up, conv_w8, conv_b, state8, w_down, tb, tl, tf):
    b, l, d = x.shape
    d_ff = w_down.shape[0]
    nj = d_ff // tf
    cb = conv_b.reshape(1, 2 * d_ff)
    lo = lambda i, t, j: (0, j)
    hi = lambda i, t, j: (0, nj + j)
    return pl.pallas_call(
        _ffn_kernel,
        grid=(b // tb, l // tl, nj),
        in_specs=[pl.BlockSpec((tb, tl, d), lambda i, t, j: (i, t, 0)),
                  pl.BlockSpec((tb, N_MOD, d), lambda i, t, j: (i, 0, 0)),
                  pl.BlockSpec((1, d), lambda i, t, j: (0, 0)),
                  pl.BlockSpec((None, None, d, tf), lambda i, t, j: (j, 0, 0, 0)),
                  pl.BlockSpec((None, None, d, tf), lambda i, t, j: (j, 1, 0, 0)),
                  pl.BlockSpec((SUBLANES, tf), lo), pl.BlockSpec((SUBLANES, tf), hi),
                  pl.BlockSpec((1, tf), lo), pl.BlockSpec((1, tf), hi),
                  pl.BlockSpec((tb, SUBLANES, tf), lambda i, t, j: (i, 0, j)),
                  pl.BlockSpec((tb, SUBLANES, tf), lambda i, t, j: (i, 0, nj + j)),
                  pl.BlockSpec((tf, d), lambda i, t, j: (j, 0))],
        out_specs=[pl.BlockSpec((tb, tl, d), lambda i, t, j: (i, t, 0)),
                   pl.BlockSpec((tb, 1, SUBLANES, tf), lambda i, t, j: (i, t, 0, j)),
                   pl.BlockSpec((tb, 1, SUBLANES, tf), lambda i, t, j: (i, t, 0, j))],
        out_shape=[jax.ShapeDtypeStruct((b, l, d), F32),
                   jax.ShapeDtypeStruct((b, l // tl, SUBLANES, d_ff), F32),
                   jax.ShapeDtypeStruct((b, l // tl, SUBLANES, d_ff), F32)],
        scratch_shapes=[pltpu.VMEM((tb * tl, d), BF16),
                        pltpu.VMEM((tb * tl, tf), BF16),
                        pltpu.VMEM((tb, SUBLANES + tl, tf), F32),
                        pltpu.VMEM((tb, SUBLANES + tl, tf), F32),
                        pltpu.VMEM((nj, tb, SUBLANES, tf), F32),
                        pltpu.VMEM((nj, tb, SUBLANES, tf), F32)],
        compiler_params=_params("arbitrary", "arbitrary", "arbitrary"),
        name="conv_ffn",
    )(x, mod, norm_w.reshape(1, d), w_up, w_up, conv_w8, conv_w8, cb, cb, state8, state8, w_down)


def _pad_rows_front(x, rows):
    pad = rows - x.shape[1]
    return jnp.pad(x, ((0, 0), (pad, 0), (0, 0)))


def _layer(x, mod, conv_prev, s0, k_prev, v_prev, ffn_prev, pos0, wts, cfg):
    b, l, d = x.shape
    proj = _inproj(x, mod, wts["norm1_w"], wts["w_in"], cfg["tb"], cfg["tl_in"])

    nb = cfg["gdn_nb"]
    c = GDN_ROWS // nb
    o_d, s_new = _gdn(proj, _pad_rows_front(conv_prev, SUBLANES), s0, wts["conv_qkv_w8"],
                      wts["head_params"], wts["gdn_norm_w"], nb, c)

    chunk = min(CHUNK, l)
    if k_prev is None:
        o_a, kn = _swa(proj, proj, proj, True, wts["q_norm_w"], wts["k_norm_w"], wts["sinks"],
                       cfg["tq"], chunk, pos0)
    else:
        o_a, kn = _swa(proj, k_prev.reshape(b, WINDOW, KV_WIDTH), v_prev.reshape(b, WINDOW, KV_WIDTH), False,
                       wts["q_norm_w"], wts["k_norm_w"], wts["sinks"], cfg["tq"], chunk, pos0)

    x1 = _outproj(x, o_d, o_a, mod, wts["w_o"], cfg["tb"], cfg["tl_out"])
    y, last_g, last_u = _ffn(x1, mod, wts["norm2_w"], wts["w_up"], wts["ffn_conv_w8"], wts["ffn_conv_b"],
                             _pad_rows_front(ffn_prev, SUBLANES), wts["w_down"],
                             cfg["tb"], cfg["tl_ffn"], cfg["tf"])

    qkv_new = proj[:, l - (GDN_CONV - 1):, COL_QKV:COL_QKV + QKV_D_WIDTH]
    if l >= GDN_CONV - 1:
        conv_new = qkv_new
    else:
        conv_new = jnp.concatenate([conv_prev, qkv_new], axis=1)[:, -(GDN_CONV - 1):]
    if l >= WINDOW:
        k_cache = kn[:, l - WINDOW:]
        v_cache = proj[:, l - WINDOW:, COL_VA:COL_VA + KV_WIDTH]
    else:
        v_new = proj[:, :, COL_VA:COL_VA + KV_WIDTH]
        k_cache = jnp.concatenate([k_prev.reshape(b, WINDOW, KV_WIDTH)[:, l:], kn], axis=1)
        v_cache = jnp.concatenate([v_prev.reshape(b, WINDOW, KV_WIDTH)[:, l:], v_new], axis=1)
    k_cache = k_cache.reshape(b, WINDOW, SWA_KV_HEADS, HEAD_DIM)
    v_cache = v_cache.reshape(b, WINDOW, SWA_KV_HEADS, HEAD_DIM)
    ffn_new = jnp.concatenate([last_g[:, -1, SUBLANES - (FFN_CONV - 1):],
                               last_u[:, -1, SUBLANES - (FFN_CONV - 1):]], axis=-1)
    return y, conv_new, s_new, k_cache, v_cache, ffn_new


def _prep_weights(l, ada_w, ada_b, norm1_w, norm2_w, w_in, conv_qkv_w, a_log, dt_bias, gdn_norm_w,
                  q_norm_w, k_norm_w, sinks, w_o, w_up, ffn_conv_w, ffn_conv_b, w_down):
    d = w_in.shape[1]
    ab0 = QKV_D_WIDTH + GDN_WIDTH
    ab1 = ab0 + 2 * GDN_HEADS
    w = w_in[l]
    w_in_r = jnp.concatenate(
        [w[:, :ab0], w[:, ab1:], w[:, ab0:ab1], jnp.zeros((d, LANES - 2 * GDN_HEADS), w.dtype)], axis=1).astype(BF16)
    assert w_in_r.shape[1] == PROJ_WIDTH
    hp = jnp.zeros((SUBLANES, LANES), F32)
    hp = hp.at[0, :GDN_HEADS].set(a_log[l].astype(F32)).at[1, :GDN_HEADS].set(dt_bias[l].astype(F32))
    return {
        "ada_w": ada_w[l], "ada_b": ada_b[l], "norm1_w": norm1_w[l], "norm2_w": norm2_w[l],
        "w_in": w_in_r.reshape(d, -1, INPROJ_TN).transpose(1, 0, 2),
        "conv_qkv_w8": jnp.pad(conv_qkv_w[l], ((0, SUBLANES - GDN_CONV), (0, 0))),
        "head_params": hp,
        "gdn_norm_w": gdn_norm_w[l], "q_norm_w": q_norm_w[l], "k_norm_w": k_norm_w[l], "sinks": sinks[l],
        "w_o": w_o[l].astype(BF16),
        "w_up": w_up[l].astype(BF16).reshape(d, 2, -1, FFN_TF).transpose(2, 1, 0, 3),
        "ffn_conv_w8": jnp.pad(ffn_conv_w[l], ((0, SUBLANES - FFN_CONV), (0, 0))),
        "ffn_conv_b": ffn_conv_b[l], "w_down": w_down[l].astype(BF16),
    }


PROMPT_CFG = dict(tb=1, tl_in=1024, gdn_nb=1, tq=128, tl_out=512, tl_ffn=512, tf=FFN_TF)


def _sample_cfg(b, l):
    return dict(tb=b, tl_in=l, gdn_nb=GDN_ROWS // l, tq=l, tl_out=l, tl_ffn=l, tf=FFN_TF)


def kernel(x_prompt, x_sample, state_conv_qkv, state_delta, cache_swa_k, cache_swa_v, state_ffn_conv,
           c_prompt, c_sample, ada_w, ada_b, norm1_w, norm2_w, w_in, conv_qkv_w, a_log, dt_bias,
           gdn_norm_w, q_norm_w, k_norm_w, sinks, w_o, w_up, ffn_conv_w, ffn_conv_b, w_down):
    depth = w_in.shape[0]
    bp, lp, d = x_prompt.shape
    bs, ls, _ = x_sample.shape
    d_ff = w_down.shape[1]
    xp, xs = x_prompt, x_sample
    acc_p = [[] for _ in range(5)]
    acc_s = [[] for _ in range(5)]
    c_all = jnp.concatenate([c_prompt, c_sample], axis=0)
    c_rows = -(-c_all.shape[0] // SUBLANES) * SUBLANES
    c_all = jnp.pad(c_all, ((0, c_rows - c_all.shape[0]), (0, 0)))
    for l in range(depth):
        wts = _prep_weights(l, ada_w, ada_b, norm1_w, norm2_w, w_in, conv_qkv_w, a_log, dt_bias, gdn_norm_w,
                            q_norm_w, k_norm_w, sinks, w_o, w_up, ffn_conv_w, ffn_conv_b, w_down)
        mod = _modulation(c_all, wts["ada_w"], wts["ada_b"]).reshape(c_rows, N_MOD, d)
        mod_p, mod_s = mod[:bp], mod[bp:bp + bs]
        dt = xp.dtype
        zero_state = (jnp.zeros((bp, GDN_CONV - 1, QKV_D_WIDTH), dt),
                      jnp.zeros((bp, GDN_HEADS, HEAD_DIM, HEAD_DIM), dt),
                      None, None,
                      jnp.zeros((bp, FFN_CONV - 1, 2 * d_ff), dt))
        xp, *new_p = _layer(xp, mod_p, *zero_state, 0, wts, PROMPT_CFG)
        xs, *new_s = _layer(xs, mod_s, state_conv_qkv[l], state_delta[l], cache_swa_k[l], cache_swa_v[l],
                            state_ffn_conv[l], PAST_LEN, wts, _sample_cfg(bs, ls))
        for acc, t in zip(acc_p, new_p):
            acc.append(t)
        for acc, t in zip(acc_s, new_s):
            acc.append(t)
    outs_p = [jnp.stack(a) for a in acc_p]
    outs_s = [jnp.stack(a) for a in acc_s]
    return (xp, xs, *outs_p, *outs_s)
```

```python
import functools
import math

import jax
import jax.numpy as jnp
from jax import lax
from jax.experimental import pallas as pl
from jax.experimental.pallas import tpu as pltpu

F32 = jnp.float32
BF16 = jnp.bfloat16

HEAD_DIM = 128
GDN_HEADS = 8
GDN_CONV = 4
SWA_HEADS = 8
SWA_KV_HEADS = 2
SWA_GROUP = SWA_HEADS // SWA_KV_HEADS
WINDOW = 128
FFN_CONV = 3
N_MOD = 6
EPS = 1e-6
CHUNK = 64
PAST_LEN = 4096

GDN_WIDTH = GDN_HEADS * HEAD_DIM
SWA_WIDTH = SWA_HEADS * HEAD_DIM
KV_WIDTH = SWA_KV_HEADS * HEAD_DIM
QKV_D_WIDTH = 3 * GDN_WIDTH

LANES = 128
SUBLANES = 8
GDN_ROWS = 128
INV_BASE = 16
VMEM_LIMIT = 61 * 1024 * 1024
FFN_COLS = 256
FFN_ROWS = 32
FFN_TF = 512
INPROJ_TN = 1152

COL_QKV = 0
COL_Z = QKV_D_WIDTH
COL_QA = COL_Z + GDN_WIDTH
COL_KA = COL_QA + SWA_WIDTH
COL_VA = COL_KA + KV_WIDTH
COL_AB = COL_VA + KV_WIDTH
PROJ_WIDTH = COL_AB + LANES


def _dot(a, b):
    return jnp.dot(a.astype(BF16), b.astype(BF16), preferred_element_type=F32)


def _dot_nt(a, b):
    return lax.dot_general(a.astype(BF16), b.astype(BF16), (((1,), (1,)), ((), ())),
                           preferred_element_type=F32)


def _silu(x):
    return x * jax.nn.sigmoid(x)


def _softplus(x):
    return jnp.maximum(x, 0.0) + jnp.log1p(jnp.exp(-jnp.abs(x)))


def _params(*sem):
    return pltpu.CompilerParams(dimension_semantics=sem, vmem_limit_bytes=VMEM_LIMIT)


def _mod_kernel(c_ref, w_ref, b_ref, o_ref):
    o_ref[...] = _dot(_silu(c_ref[...]), w_ref[...]) + b_ref[...]


def _modulation(c, ada_w, ada_b, tn=1024):
    rows, d = c.shape
    n = ada_w.shape[1]
    return pl.pallas_call(
        _mod_kernel,
        grid=(n // tn,),
        in_specs=[pl.BlockSpec((rows, d), lambda j: (0, 0)),
                  pl.BlockSpec((d, tn), lambda j: (0, j)),
                  pl.BlockSpec((1, tn), lambda j: (0, j))],
        out_specs=pl.BlockSpec((rows, tn), lambda j: (0, j)),
        out_shape=jax.ShapeDtypeStruct((rows, n), F32),
        compiler_params=_params("arbitrary"),
        name="modulation",
    )(c, ada_w, ada_b.reshape(1, n))


def _modulated_norm(x, nw, shift, scale):
    y = x * lax.rsqrt(jnp.mean(x * x, axis=-1, keepdims=True) + EPS) * nw
    return y * (1.0 + scale) + shift


def _inproj_kernel(x_ref, mod_ref, nw_ref, w_ref, o_ref, h_ref):
    tb, tl, d = x_ref.shape

    @pl.when(pl.program_id(2) == 0)
    def _():
        h = _modulated_norm(x_ref[...], nw_ref[...], mod_ref[:, 0:1, :], mod_ref[:, 1:2, :])
        h_ref[...] = h.reshape(tb * tl, d).astype(BF16)

    o_ref[...] = jnp.dot(h_ref[...], w_ref[...], preferred_element_type=F32).reshape(o_ref.shape)


def _inproj(x, mod, norm_w, w, tb, tl):
    b, l, d = x.shape
    n = w.shape[1]
    tn = INPROJ_TN
    return pl.pallas_call(
        _inproj_kernel,
        grid=(b // tb, l // tl, n // tn),
        in_specs=[pl.BlockSpec((tb, tl, d), lambda i, t, j: (i, t, 0)),
                  pl.BlockSpec((tb, N_MOD, d), lambda i, t, j: (i, 0, 0)),
                  pl.BlockSpec((1, d), lambda i, t, j: (0, 0)),
                  pl.BlockSpec((d, tn), lambda i, t, j: (0, j))],
        out_specs=pl.BlockSpec((tb, tl, tn), lambda i, t, j: (i, t, j)),
        out_shape=jax.ShapeDtypeStruct((b, l, n), F32),
        scratch_shapes=[pltpu.VMEM((tb * tl, d), BF16)],
        compiler_params=_params("arbitrary", "arbitrary", "arbitrary"),
        name="in_projection",
    )(x, mod, norm_w.reshape(1, d), w)


def _unit_lower_inverse(a_list, same_block, eye):
    sizes = sorted(same_block)
    p = [-jnp.where(same_block[sizes[0]], a, 0.0) for a in a_list]
    x = [eye + pi for pi in p]
    k = 2
    while k < sizes[0]:
        p = [_dot(pi, pi) for pi in p]
        x = [xi + _dot(xi, pi) for xi, pi in zip(x, p)]
        k *= 2
    for s_prev, s in zip(sizes[:-1], sizes[1:]):
        off = jnp.logical_and(same_block[s], jnp.logical_not(same_block[s_prev]))
        xa = [_dot(xi, jnp.where(off, a, 0.0)) for xi, a in zip(x, a_list)]
        x = [xi - _dot(xai, xi) for xi, xai in zip(x, xa)]
    return x


def _gdn_kernel(qkv_ref, z_ref, ab_ref, cst_ref, s0_ref, cw_ref, hp_ref, nw_ref,
                o_ref, sout_ref, ext_ref, s_ref):
    nb, c, _ = qkv_ref.shape
    r = nb * c
    t = pl.program_id(1)

    @pl.when(t == 0)
    def _():
        ext_ref[:, 0:SUBLANES, :] = cst_ref[...]
        s_ref[...] = s0_ref[...]

    ext_ref[:, SUBLANES:SUBLANES + c, :] = qkv_ref[...]

    ab = ab_ref[...].reshape(r, LANES)
    hp = hp_ref[...]
    g = -jnp.exp(hp[0:1, :]) * _softplus(ab + hp[1:2, :])
    beta = jax.nn.sigmoid(ab)
    pos = jnp.bitwise_and(lax.broadcasted_iota(jnp.int32, (r, LANES), 0), c - 1)
    gc = g
    s = 1
    while s < c:
        gc = gc + jnp.where(pos >= s, pltpu.roll(gc, s, 0), 0.0)
        s *= 2
    if nb == 1:
        gcl = jnp.broadcast_to(gc[r - 1:r, :], (r, LANES))
    else:
        gcl = jnp.where(pos == c - 1, gc, 0.0)
        s = 1
        while s < c:
            gcl = gcl + jnp.where(pos < c - s, pltpu.roll(gcl, r - s, 0), 0.0)
            s *= 2
    eg = jnp.exp(gc)
    egl = jnp.exp(gcl - gc)
    gl_rows = [jnp.exp(gc[sgi * c + c - 1:sgi * c + c, :]) for sgi in range(nb)]
    gct = gc.T

    row = lax.broadcasted_iota(jnp.int32, (r, r), 0)
    col = lax.broadcasted_iota(jnp.int32, (r, r), 1)

    def same(sz):
        sh = int(math.log2(sz))
        return jnp.right_shift(row, sh) == jnp.right_shift(col, sh)

    sizes = []
    sz = min(INV_BASE, c)
    while sz <= c:
        sizes.append(sz)
        sz *= 2
    same_block = {sz: same(sz) for sz in sizes}
    seg = same_block[c]
    incl = jnp.logical_and(seg, row >= col)
    strict = jnp.logical_and(seg, row > col)
    eye = (row == col).astype(F32)
    colseg = jnp.right_shift(lax.broadcasted_iota(jnp.int32, (HEAD_DIM, r), 1), int(math.log2(c)))

    cw = cw_ref[...]

    def conv_cols(c0):
        cols = slice(c0, c0 + HEAD_DIM)
        if nb == 1:
            full = ext_ref[0, :, cols]
            acc = cw[GDN_CONV - 1:GDN_CONV, cols] * full[SUBLANES:]
            for i in range(GDN_CONV - 1):
                acc = acc + cw[i:i + 1, cols] * pltpu.roll(full, GDN_CONV - 1 - i, 0)[SUBLANES:]
            return _silu(acc)
        acc = cw[0:1, cols] * ext_ref[:, SUBLANES - 3:SUBLANES - 3 + c, cols]
        for i in range(1, GDN_CONV):
            lo = SUBLANES - 3 + i
            acc = acc + cw[i:i + 1, cols] * ext_ref[:, lo:lo + c, cols]
        return _silu(acc).reshape(r, HEAD_DIM)

    nw = nw_ref[...]
    heads = range(GDN_HEADS)
    q = [conv_cols(h * HEAD_DIM) for h in heads]
    k = [conv_cols(GDN_WIDTH + h * HEAD_DIM) for h in heads]
    v = [conv_cols(2 * GDN_WIDTH + h * HEAD_DIM) for h in heads]
    qn = [x * (lax.rsqrt(jnp.sum(x * x, axis=-1, keepdims=True) + EPS) * (HEAD_DIM ** -0.5)) for x in q]
    kn = [x * lax.rsqrt(jnp.sum(x * x, axis=-1, keepdims=True) + EPS) for x in k]
    bcol = [beta[:, SUBLANES + h:SUBLANES + h + 1] for h in heads]
    egcol = [eg[:, h:h + 1] for h in heads]
    dec = [jnp.exp(jnp.where(incl, gc[:, h:h + 1] - gct[h:h + 1, :], -jnp.inf)) for h in heads]
    kb = [kn[h] * bcol[h] for h in heads]
    kn16 = [x.astype(BF16) for x in kn]
    kq = [_dot_nt(jnp.concatenate([kb[h].astype(BF16), qn[h].astype(BF16)], axis=0), kn16[h]) for h in heads]
    a = [jnp.where(strict, kq[h][:r] * dec[h], 0.0) for h in heads]
    qk16 = [(kq[h][r:] * dec[h]).astype(BF16) for h in heads]
    tinv = _unit_lower_inverse(a, same_block, eye)
    sol = [_dot(tinv[h], jnp.concatenate([v[h] * bcol[h], kb[h] * egcol[h]], axis=1)) for h in heads]
    qg = [qn[h] * egcol[h] for h in heads]
    kdt16 = [(kn[h] * egl[:, h:h + 1]).T.astype(BF16) for h in heads]

    ws, qs = [], []
    for h in heads:
        w = sol[h][:, HEAD_DIM:]
        ws_parts, qs_parts = [], []
        for sgi in range(nb):
            lo = sgi * c
            wq = jnp.concatenate([w[lo:lo + c], qg[h][lo:lo + c]], axis=0)
            res = _dot(wq, s_ref[sgi, h])
            ws_parts.append(res[:c])
            qs_parts.append(res[c:])
        ws.append(ws_parts[0] if nb == 1 else jnp.concatenate(ws_parts, axis=0))
        qs.append(qs_parts[0] if nb == 1 else jnp.concatenate(qs_parts, axis=0))
    v16 = [(sol[h][:, :HEAD_DIM] - ws[h]).astype(BF16) for h in heads]
    o = [qs[h] + jnp.dot(qk16[h], v16[h], preferred_element_type=F32) for h in heads]
    for h in heads:
        for sgi in range(nb):
            if nb == 1:
                kdm = kdt16[h]
            else:
                kdm = jnp.where(colseg == sgi, kdt16[h], jnp.zeros_like(kdt16[h]))
            s_ref[sgi, h] = (s_ref[sgi, h] * gl_rows[sgi][:, h:h + 1]
                             + jnp.dot(kdm, v16[h], preferred_element_type=F32))
    for h in heads:
        on = o[h] * lax.rsqrt(jnp.mean(o[h] * o[h], axis=-1, keepdims=True) + EPS) * nw
        zh = z_ref[:, :, h * HEAD_DIM:(h + 1) * HEAD_DIM].reshape(r, HEAD_DIM)
        out = on * _silu(zh)
        o_ref[:, :, h * HEAD_DIM:(h + 1) * HEAD_DIM] = out.reshape(nb, c, HEAD_DIM).astype(o_ref.dtype)

    ext_ref[:, 0:SUBLANES, :] = ext_ref[:, c:c + SUBLANES, :]

    @pl.when(t == pl.num_programs(1) - 1)
    def _():
        sout_ref[...] = s_ref[...]


def _gdn(proj, conv_state8, s0, conv_w8, head_params, norm_w, nb, c):
    b, l, _ = proj.shape
    nt = l // c
    zb = COL_Z // GDN_WIDTH
    abb = COL_AB // LANES
    return pl.pallas_call(
        _gdn_kernel,
        grid=(b // nb, nt),
        in_specs=[pl.BlockSpec((nb, c, QKV_D_WIDTH), lambda i, t: (i, t, 0)),
                  pl.BlockSpec((nb, c, GDN_WIDTH), lambda i, t: (i, t, zb)),
                  pl.BlockSpec((nb, c, LANES), lambda i, t: (i, t, abb)),
                  pl.BlockSpec((nb, SUBLANES, QKV_D_WIDTH), lambda i, t: (i, 0, 0)),
                  pl.BlockSpec((nb, GDN_HEADS, HEAD_DIM, HEAD_DIM), lambda i, t: (i, 0, 0, 0)),
                  pl.BlockSpec((SUBLANES, QKV_D_WIDTH), lambda i, t: (0, 0)),
                  pl.BlockSpec((SUBLANES, LANES), lambda i, t: (0, 0)),
                  pl.BlockSpec((1, HEAD_DIM), lambda i, t: (0, 0))],
        out_specs=[pl.BlockSpec((nb, c, GDN_WIDTH), lambda i, t: (i, t, 0)),
                   pl.BlockSpec((nb, GDN_HEADS, HEAD_DIM, HEAD_DIM), lambda i, t: (i, 0, 0, 0))],
        out_shape=[jax.ShapeDtypeStruct((b, l, GDN_WIDTH), BF16),
                   jax.ShapeDtypeStruct((b, GDN_HEADS, HEAD_DIM, HEAD_DIM), F32)],
        scratch_shapes=[pltpu.VMEM((nb, SUBLANES + c, QKV_D_WIDTH), F32),
                        pltpu.VMEM((nb, GDN_HEADS, HEAD_DIM, HEAD_DIM), F32)],
        compiler_params=_params("arbitrary", "arbitrary"),
        name="gated_delta_rule",
    )(proj, proj, proj, conv_state8, s0, conv_w8, head_params, norm_w.reshape(1, HEAD_DIM))


def _head_rms(x, w):
    return x * lax.rsqrt(jnp.mean(x * x, axis=-1, keepdims=True) + EPS) * w


def _swa_kernel(sink_ref, q_ref, k_ref, v_ref, kp_ref, vp_ref, qw_ref, kw_ref, o_ref, kn_ref,
                *, chunk, pos0, norm_prev):
    tq = q_ref.shape[1]
    tk = max(tq, LANES)
    m = SWA_GROUP * tq
    t0 = pos0 + pl.program_id(1) * tq
    qw = qw_ref[...]
    kw = kw_ref[...]

    rown = lax.broadcasted_iota(jnp.int32, (m, 1), 0)
    grp = rown // tq
    qi = rown - grp * tq
    cs = (qi // chunk) * chunk
    kj = lax.broadcasted_iota(jnp.int32, (1, tk), 1)
    pj = lax.broadcasted_iota(jnp.int32, (1, WINDOW), 1)
    vis_own = jnp.logical_and(kj < cs + chunk, kj < tq)
    vis_prev = jnp.logical_and(pj - WINDOW >= cs - WINDOW, t0 - WINDOW + pj >= 0)
    dist_own = jnp.abs(qi - kj).astype(F32)
    dist_prev = (qi + WINDOW - pj).astype(F32)
    scale = HEAD_DIM ** -0.5

    for kvh in range(SWA_KV_HEADS):
        slope = jnp.zeros((m, 1), F32)
        sink = jnp.zeros((m, 1), F32)
        for gi in range(SWA_GROUP):
            hq = kvh * SWA_GROUP + gi
            slope = jnp.where(grp == gi, 2.0 ** (-8.0 * (hq + 1) / SWA_HEADS), slope)
            sink = jnp.where(grp == gi, sink_ref[hq], sink)
        qs = jnp.concatenate(
            [_head_rms(q_ref[0, :, (kvh * SWA_GROUP + gi) * HEAD_DIM:(kvh * SWA_GROUP + gi + 1) * HEAD_DIM], qw)
             for gi in range(SWA_GROUP)], axis=0)
        ksl = slice(kvh * HEAD_DIM, (kvh + 1) * HEAD_DIM)
        k_own = _head_rms(k_ref[0, :, ksl], kw)
        kn_ref[0, :, ksl] = k_own
        v_own = v_ref[0, :, ksl]
        if tk > tq:
            pad = jnp.zeros((tk - tq, HEAD_DIM), F32)
            k_own = jnp.concatenate([k_own, pad], axis=0)
            v_own = jnp.concatenate([v_own, pad], axis=0)
        k_prev = kp_ref[0, :, ksl]
        if norm_prev:
            k_prev = _head_rms(k_prev, kw)
        v_prev = vp_ref[0, :, ksl]

        s_own = jnp.where(vis_own, _dot_nt(qs, k_own) * scale - slope * dist_own, -jnp.inf)
        s_prev = jnp.where(vis_prev, _dot_nt(qs, k_prev) * scale - slope * dist_prev, -jnp.inf)
        mx = jnp.maximum(jnp.maximum(jnp.max(s_own, axis=-1, keepdims=True),
                                     jnp.max(s_prev, axis=-1, keepdims=True)), sink)
        p_own = jnp.exp(s_own - mx)
        p_prev = jnp.exp(s_prev - mx)
        den = (jnp.sum(p_own, axis=-1, keepdims=True) + jnp.sum(p_prev, axis=-1, keepdims=True)
               + jnp.exp(sink - mx))
        inv = 1.0 / den
        o = _dot(p_own * inv, v_own) + _dot(p_prev * inv, v_prev)
        for gi in range(SWA_GROUP):
            hq = kvh * SWA_GROUP + gi
            o_ref[0, :, hq * HEAD_DIM:(hq + 1) * HEAD_DIM] = o[gi * tq:(gi + 1) * tq].astype(o_ref.dtype)


def _swa(proj, k_prev_src, v_prev_src, prev_from_proj, q_norm_w, k_norm_w, sinks, tq, chunk, pos0):
    b, l, _ = proj.shape
    qb = COL_QA // SWA_WIDTH
    kb = COL_KA // KV_WIDTH
    vb = COL_VA // KV_WIDTH
    if prev_from_proj:
        assert tq == WINDOW
        kp_spec = pl.BlockSpec((1, WINDOW, KV_WIDTH), lambda i, t: (i, jnp.maximum(t - 1, 0), kb))
        vp_spec = pl.BlockSpec((1, WINDOW, KV_WIDTH), lambda i, t: (i, jnp.maximum(t - 1, 0), vb))
    else:
        assert l == tq
        kp_spec = pl.BlockSpec((1, WINDOW, KV_WIDTH), lambda i, t: (i, 0, 0))
        vp_spec = pl.BlockSpec((1, WINDOW, KV_WIDTH), lambda i, t: (i, 0, 0))
    kernel = functools.partial(_swa_kernel, chunk=chunk, pos0=pos0, norm_prev=prev_from_proj)
    return pl.pallas_call(
        kernel,
        grid=(b, l // tq),
        in_specs=[pl.BlockSpec(memory_space=pltpu.SMEM),
                  pl.BlockSpec((1, tq, SWA_WIDTH), lambda i, t: (i, t, qb)),
                  pl.BlockSpec((1, tq, KV_WIDTH), lambda i, t: (i, t, kb)),
                  pl.BlockSpec((1, tq, KV_WIDTH), lambda i, t: (i, t, vb)),
                  kp_spec, vp_spec,
                  pl.BlockSpec((1, HEAD_DIM), lambda i, t: (0, 0)),
                  pl.BlockSpec((1, HEAD_DIM), lambda i, t: (0, 0))],
        out_specs=[pl.BlockSpec((1, tq, SWA_WIDTH), lambda i, t: (i, t, 0)),
                   pl.BlockSpec((1, tq, KV_WIDTH), lambda i, t: (i, t, 0))],
        out_shape=[jax.ShapeDtypeStruct((b, l, SWA_WIDTH), BF16),
                   jax.ShapeDtypeStruct((b, l, KV_WIDTH), F32)],
        compiler_params=_params("arbitrary", "arbitrary"),
        name="sliding_window_attention",
    )(sinks, proj, proj, proj, k_prev_src, v_prev_src,
      q_norm_w.reshape(1, HEAD_DIM), k_norm_w.reshape(1, HEAD_DIM))


def _outproj_kernel(x_ref, od_ref, oa_ref, mod_ref, w_ref, o_ref):
    tb, tl, d = x_ref.shape
    od = od_ref[...].reshape(tb * tl, GDN_WIDTH)
    oa = oa_ref[...].reshape(tb * tl, SWA_WIDTH)
    acc = (jnp.dot(od, w_ref[0:GDN_WIDTH, :], preferred_element_type=F32)
           + jnp.dot(oa, w_ref[GDN_WIDTH:GDN_WIDTH + SWA_WIDTH, :], preferred_element_type=F32))
    o_ref[...] = x_ref[...] + mod_ref[:, 2:3, :] * acc.reshape(tb, tl, d)


def _outproj(x, o_d, o_a, mod, w_o, tb, tl):
    b, l, d = x.shape
    return pl.pallas_call(
        _outproj_kernel,
        grid=(b // tb, l // tl),
        in_specs=[pl.BlockSpec((tb, tl, d), lambda i, t: (i, t, 0)),
                  pl.BlockSpec((tb, tl, GDN_WIDTH), lambda i, t: (i, t, 0)),
                  pl.BlockSpec((tb, tl, SWA_WIDTH), lambda i, t: (i, t, 0)),
                  pl.BlockSpec((tb, N_MOD, d), lambda i, t: (i, 0, 0)),
                  pl.BlockSpec(w_o.shape, lambda i, t: (0, 0))],
        out_specs=pl.BlockSpec((tb, tl, d), lambda i, t: (i, t, 0)),
        out_shape=jax.ShapeDtypeStruct((b, l, d), F32),
        compiler_params=_params("arbitrary", "arbitrary"),
        name="out_projection",
    )(x, o_d, o_a, mod, w_o)


def _ffn_kernel(x_ref, mod_ref, nw_ref, wg_ref, wu_ref, cwg_ref, cwu_ref, bg_ref, bu_ref,
                stg_ref, stu_ref, wd_ref, y_ref, lastg_ref, lastu_ref,
                h_ref, act_ref, extg_ref, extu_ref, carg_ref, caru_ref):
    tb, tl, d = x_ref.shape
    tf = wg_ref.shape[1]
    t = pl.program_id(1)
    j = pl.program_id(2)

    halves = ((wg_ref, cwg_ref, bg_ref, stg_ref, extg_ref, carg_ref, lastg_ref),
              (wu_ref, cwu_ref, bu_ref, stu_ref, extu_ref, caru_ref, lastu_ref))

    @pl.when(j == 0)
    def _():
        h = _modulated_norm(x_ref[...], nw_ref[...], mod_ref[:, 3:4, :], mod_ref[:, 4:5, :])
        h_ref[...] = h.reshape(tb * tl, d).astype(BF16)
        y_ref[...] = jnp.zeros_like(y_ref)

    @pl.when(t == 0)
    def _():
        for _, _, _, st_ref, ext_ref, _, _ in halves:
            ext_ref[:, 0:SUBLANES, :] = st_ref[...]

    @pl.when(t > 0)
    def _():
        for _, _, _, _, ext_ref, car_ref, _ in halves:
            ext_ref[:, 0:SUBLANES, :] = car_ref[j]

    h = h_ref[...]
    ncb = tf // FFN_COLS
    for cb in range(ncb):
        cs = slice(cb * FFN_COLS, (cb + 1) * FFN_COLS)
        for w_ref, _, _, _, ext_ref, _, _ in halves:
            u = jnp.dot(h, w_ref[:, cs], preferred_element_type=F32)
            ext_ref[:, SUBLANES:SUBLANES + tl, cs] = u.reshape(tb, tl, FFN_COLS)
    for _, _, _, _, ext_ref, car_ref, last_ref in halves:
        tail = ext_ref[:, tl:tl + SUBLANES, :]
        car_ref[j] = tail
        last_ref[:, 0] = tail

    if tb == 1:
        chunks = [(slice(0, 1), r0, FFN_ROWS) for r0 in range(0, tl, FFN_ROWS)]
    else:
        nbc = max(1, FFN_ROWS // tl)
        chunks = [(slice(b0, b0 + nbc), 0, tl) for b0 in range(0, tb, nbc)]
    for cb in range(ncb):
        cs = slice(cb * FFN_COLS, (cb + 1) * FFN_COLS)
        for bs, r0, nr in chunks:
            conv = []
            for _, cw_ref, b_ref, _, ext_ref, _, _ in halves:
                acc = b_ref[:, cs]
                for i in range(FFN_CONV):
                    lo = SUBLANES - (FFN_CONV - 1) + i + r0
                    acc = acc + cw_ref[i:i + 1, cs] * ext_ref[bs, lo:lo + nr, cs]
                conv.append(acc)
            act = _silu(conv[0]) * conv[1]
            nbs = bs.stop - bs.start
            row0 = bs.start * tl + r0
            act_ref[row0:row0 + nbs * nr, cs] = act.reshape(nbs * nr, FFN_COLS).astype(BF16)

    y_ref[...] += jnp.dot(act_ref[...], wd_ref[...], preferred_element_type=F32).reshape(tb, tl, d)

    @pl.when(j == pl.num_programs(2) - 1)
    def _():
        y_ref[...] = x_ref[...] + mod_ref[:, 5:6, :] * y_ref[...]


def _ffn(x, mod, norm_w, w_up, conv_w8, conv_b, state8, w_down, tb, tl, tf):
    b, l, d = x.shape
    d_ff = w_down.shape[0]
    nj = d_ff // tf
    cb = conv_b.reshape(1, 2 * d_ff)
    lo = lambda i, t, j: (0, j)
    hi = lambda i, t, j: (0, nj + j)
    return pl.pallas_call(
        _ffn_kernel,
        grid=(b // tb, l // tl, nj),
        in_specs=[pl.BlockSpec((tb, tl, d), lambda i, t, j: (i, t, 0), pipeline_mode=pl.Buffered(1)),
                  pl.BlockSpec((tb, N_MOD, d), lambda i, t, j: (i, 0, 0)),
                  pl.BlockSpec((1, d), lambda i, t, j: (0, 0)),
                  pl.BlockSpec((d, tf), lo), pl.BlockSpec((d, tf), hi),
                  pl.BlockSpec((SUBLANES, tf), lo), pl.BlockSpec((SUBLANES, tf), hi),
                  pl.BlockSpec((1, tf), lo), pl.BlockSpec((1, tf), hi),
                  pl.BlockSpec((tb, SUBLANES, tf), lambda i, t, j: (i, 0, j)),
                  pl.BlockSpec((tb, SUBLANES, tf), lambda i, t, j: (i, 0, nj + j)),
                  pl.BlockSpec((tf, d), lambda i, t, j: (j, 0))],
        out_specs=[pl.BlockSpec((tb, tl, d), lambda i, t, j: (i, t, 0)),
                   pl.BlockSpec((tb, 1, SUBLANES, tf), lambda i, t, j: (i, t, 0, j)),
                   pl.BlockSpec((tb, 1, SUBLANES, tf), lambda i, t, j: (i, t, 0, j))],
        out_shape=[jax.ShapeDtypeStruct((b, l, d), F32),
                   jax.ShapeDtypeStruct((b, l // tl, SUBLANES, d_ff), F32),
                   jax.ShapeDtypeStruct((b, l // tl, SUBLANES, d_ff), F32)],
        scratch_shapes=[pltpu.VMEM((tb * tl, d), BF16),
                        pltpu.VMEM((tb * tl, tf), BF16),
                        pltpu.VMEM((tb, SUBLANES + tl, tf), F32),
                        pltpu.VMEM((tb, SUBLANES + tl, tf), F32),
                        pltpu.VMEM((nj, tb, SUBLANES, tf), F32),
                        pltpu.VMEM((nj, tb, SUBLANES, tf), F32)],
        compiler_params=_params("arbitrary", "arbitrary", "arbitrary"),
        name="conv_ffn",
    )(x, mod, norm_w.reshape(1, d), w_up, w_up, conv_w8, conv_w8, cb, cb, state8, state8, w_down)


def _pad_rows_front(x, rows):
    pad = rows - x.shape[1]
    return jnp.pad(x, ((0, 0), (pad, 0), (0, 0)))


def _layer(x, mod, conv_prev, s0, k_prev, v_prev, ffn_prev, pos0, wts, cfg):
    b, l, d = x.shape
    proj = _inproj(x, mod, wts["norm1_w"], wts["w_in"], cfg["tb"], cfg["tl_in"])

    nb = cfg["gdn_nb"]
    c = GDN_ROWS // nb
    o_d, s_new = _gdn(proj, _pad_rows_front(conv_prev, SUBLANES), s0, wts["conv_qkv_w8"],
                      wts["head_params"], wts["gdn_norm_w"], nb, c)

    chunk = min(CHUNK, l)
    if k_prev is None:
        o_a, kn = _swa(proj, proj, proj, True, wts["q_norm_w"], wts["k_norm_w"], wts["sinks"],
                       cfg["tq"], chunk, pos0)
    else:
        o_a, kn = _swa(proj, k_prev.reshape(b, WINDOW, KV_WIDTH), v_prev.reshape(b, WINDOW, KV_WIDTH), False,
                       wts["q_norm_w"], wts["k_norm_w"], wts["sinks"], cfg["tq"], chunk, pos0)

    x1 = _outproj(x, o_d, o_a, mod, wts["w_o"], cfg["tb"], cfg["tl_out"])
    y, last_g, last_u = _ffn(x1, mod, wts["norm2_w"], wts["w_up"], wts["ffn_conv_w8"], wts["ffn_conv_b"],
                             _pad_rows_front(ffn_prev, SUBLANES), wts["w_down"],
                             cfg["tb"], cfg["tl_ffn"], cfg["tf"])

    qkv_new = proj[:, l - (GDN_CONV - 1):, COL_QKV:COL_QKV + QKV_D_WIDTH]
    if l >= GDN_CONV - 1:
        conv_new = qkv_new
    else:
        conv_new = jnp.concatenate([conv_prev, qkv_new], axis=1)[:, -(GDN_CONV - 1):]
    if l >= WINDOW:
        k_cache = kn[:, l - WINDOW:]
        v_cache = proj[:, l - WINDOW:, COL_VA:COL_VA + KV_WIDTH]
    else:
        v_new = proj[:, :, COL_VA:COL_VA + KV_WIDTH]
        k_cache = jnp.concatenate([k_prev.reshape(b, WINDOW, KV_WIDTH)[:, l:], kn], axis=1)
        v_cache = jnp.concatenate([v_prev.reshape(b, WINDOW, KV_WIDTH)[:, l:], v_new], axis=1)
    k_cache = k_cache.reshape(b, WINDOW, SWA_KV_HEADS, HEAD_DIM)
    v_cache = v_cache.reshape(b, WINDOW, SWA_KV_HEADS, HEAD_DIM)
    ffn_new = jnp.concatenate([last_g[:, -1, SUBLANES - (FFN_CONV - 1):],
                               last_u[:, -1, SUBLANES - (FFN_CONV - 1):]], axis=-1)
    return y, conv_new, s_new, k_cache, v_cache, ffn_new


def _prep_weights(l, ada_w, ada_b, norm1_w, norm2_w, w_in, conv_qkv_w, a_log, dt_bias, gdn_norm_w,
                  q_norm_w, k_norm_w, sinks, w_o, w_up, ffn_conv_w, ffn_conv_b, w_down):
    d = w_in.shape[1]
    ab0 = QKV_D_WIDTH + GDN_WIDTH
    ab1 = ab0 + 2 * GDN_HEADS
    w = w_in[l]
    w_in_r = jnp.concatenate(
        [w[:, :ab0], w[:, ab1:], w[:, ab0:ab1], jnp.zeros((d, LANES - 2 * GDN_HEADS), w.dtype)], axis=1).astype(BF16)
    assert w_in_r.shape[1] == PROJ_WIDTH
    hp = jnp.zeros((SUBLANES, LANES), F32)
    hp = hp.at[0, :GDN_HEADS].set(a_log[l].astype(F32)).at[1, :GDN_HEADS].set(dt_bias[l].astype(F32))
    return {
        "ada_w": ada_w[l], "ada_b": ada_b[l], "norm1_w": norm1_w[l], "norm2_w": norm2_w[l],
        "w_in": w_in_r,
        "conv_qkv_w8": jnp.pad(conv_qkv_w[l], ((0, SUBLANES - GDN_CONV), (0, 0))),
        "head_params": hp,
        "gdn_norm_w": gdn_norm_w[l], "q_norm_w": q_norm_w[l], "k_norm_w": k_norm_w[l], "sinks": sinks[l],
        "w_o": w_o[l].astype(BF16), "w_up": w_up[l].astype(BF16),
        "ffn_conv_w8": jnp.pad(ffn_conv_w[l], ((0, SUBLANES - FFN_CONV), (0, 0))),
        "ffn_conv_b": ffn_conv_b[l], "w_down": w_down[l].astype(BF16),
    }


PROMPT_CFG = dict(tb=1, tl_in=1024, gdn_nb=1, tq=128, tl_out=512, tl_ffn=1024, tf=FFN_TF)


def _sample_cfg(b, l):
    return dict(tb=b, tl_in=l, gdn_nb=GDN_ROWS // l, tq=l, tl_out=l, tl_ffn=l, tf=FFN_TF)


def kernel(x_prompt, x_sample, state_conv_qkv, state_delta, cache_swa_k, cache_swa_v, state_ffn_conv,
           c_prompt, c_sample, ada_w, ada_b, norm1_w, norm2_w, w_in, conv_qkv_w, a_log, dt_bias,
           gdn_norm_w, q_norm_w, k_norm_w, sinks, w_o, w_up, ffn_conv_w, ffn_conv_b, w_down):
    depth = w_in.shape[0]
    bp, lp, d = x_prompt.shape
    bs, ls, _ = x_sample.shape
    d_ff = w_down.shape[1]
    xp, xs = x_prompt, x_sample
    acc_p = [[] for _ in range(5)]
    acc_s = [[] for _ in range(5)]
    c_all = jnp.concatenate([c_prompt, c_sample], axis=0)
    c_rows = -(-c_all.shape[0] // SUBLANES) * SUBLANES
    c_all = jnp.pad(c_all, ((0, c_rows - c_all.shape[0]), (0, 0)))
    for l in range(depth):
        wts = _prep_weights(l, ada_w, ada_b, norm1_w, norm2_w, w_in, conv_qkv_w, a_log, dt_bias, gdn_norm_w,
                            q_norm_w, k_norm_w, sinks, w_o, w_up, ffn_conv_w, ffn_conv_b, w_down)
        mod = _modulation(c_all, wts["ada_w"], wts["ada_b"]).reshape(c_rows, N_MOD, d)
        mod_p, mod_s = mod[:bp], mod[bp:bp + bs]
        dt = xp.dtype
        zero_state = (jnp.zeros((bp, GDN_CONV - 1, QKV_D_WIDTH), dt),
                      jnp.zeros((bp, GDN_HEADS, HEAD_DIM, HEAD_DIM), dt),
                      None, None,
                      jnp.zeros((bp, FFN_CONV - 1, 2 * d_ff), dt))
        xp, *new_p = _layer(xp, mod_p, *zero_state, 0, wts, PROMPT_CFG)
        xs, *new_s = _layer(xs, mod_s, state_conv_qkv[l], state_delta[l], cache_swa_k[l], cache_swa_v[l],
                            state_ffn_conv[l], PAST_LEN, wts, _sample_cfg(bs, ls))
        for acc, t in zip(acc_p, new_p):
            acc.append(t)
        for acc, t in zip(acc_s, new_s):
            acc.append(t)
    outs_p = [jnp.stack(a) for a in acc_p]
    outs_s = [jnp.stack(a) for a in acc_s]
    return (xp, xs, *outs_p, *outs_s)
```

```python
import functools
import math

import jax
import jax.numpy as jnp
from jax import lax
from jax.experimental import pallas as pl
from jax.experimental.pallas import tpu as pltpu

F32 = jnp.float32
BF16 = jnp.bfloat16

HEAD_DIM = 128
GDN_HEADS = 8
GDN_CONV = 4
SWA_HEADS = 8
SWA_KV_HEADS = 2
SWA_GROUP = SWA_HEADS // SWA_KV_HEADS
WINDOW = 128
FFN_CONV = 3
N_MOD = 6
EPS = 1e-6
CHUNK = 64
PAST_LEN = 4096

GDN_WIDTH = GDN_HEADS * HEAD_DIM
SWA_WIDTH = SWA_HEADS * HEAD_DIM
KV_WIDTH = SWA_KV_HEADS * HEAD_DIM
QKV_D_WIDTH = 3 * GDN_WIDTH

LANES = 128
SUBLANES = 8
GDN_ROWS = 128
INV_BASE = 16
VMEM_LIMIT = 61 * 1024 * 1024
FFN_COLS = 512
FFN_ROWS = 32
FFN_TF = 512
INPROJ_TN = 1152

COL_QKV = 0
COL_Z = QKV_D_WIDTH
COL_QA = COL_Z + GDN_WIDTH
COL_KA = COL_QA + SWA_WIDTH
COL_VA = COL_KA + KV_WIDTH
COL_AB = COL_VA + KV_WIDTH
PROJ_WIDTH = COL_AB + LANES


def _dot(a, b):
    return jnp.dot(a.astype(BF16), b.astype(BF16), preferred_element_type=F32)


def _dot_nt(a, b):
    return lax.dot_general(a.astype(BF16), b.astype(BF16), (((1,), (1,)), ((), ())),
                           preferred_element_type=F32)


def _silu(x):
    return x * jax.nn.sigmoid(x)


def _softplus(x):
    return jnp.maximum(x, 0.0) + jnp.log1p(jnp.exp(-jnp.abs(x)))


def _params(*sem):
    return pltpu.CompilerParams(dimension_semantics=sem, vmem_limit_bytes=VMEM_LIMIT)


def _mod_kernel(c_ref, w_ref, b_ref, o_ref):
    o_ref[...] = _dot(_silu(c_ref[...]), w_ref[...]) + b_ref[...]


def _modulation(c, ada_w, ada_b, tn=1024):
    rows, d = c.shape
    n = ada_w.shape[1]
    return pl.pallas_call(
        _mod_kernel,
        grid=(n // tn,),
        in_specs=[pl.BlockSpec((rows, d), lambda j: (0, 0)),
                  pl.BlockSpec((d, tn), lambda j: (0, j)),
                  pl.BlockSpec((1, tn), lambda j: (0, j))],
        out_specs=pl.BlockSpec((rows, tn), lambda j: (0, j)),
        out_shape=jax.ShapeDtypeStruct((rows, n), F32),
        compiler_params=_params("arbitrary"),
        name="modulation",
    )(c, ada_w, ada_b.reshape(1, n))


def _modulated_norm(x, nw, shift, scale):
    y = x * lax.rsqrt(jnp.mean(x * x, axis=-1, keepdims=True) + EPS) * nw
    return y * (1.0 + scale) + shift


def _inproj_kernel(x_ref, mod_ref, nw_ref, w_ref, o_ref, h_ref):
    tb, tl, d = x_ref.shape

    @pl.when(pl.program_id(2) == 0)
    def _():
        h = _modulated_norm(x_ref[...], nw_ref[...], mod_ref[:, 0:1, :], mod_ref[:, 1:2, :])
        h_ref[...] = h.reshape(tb * tl, d).astype(BF16)

    o_ref[...] = jnp.dot(h_ref[...], w_ref[...], preferred_element_type=F32).reshape(o_ref.shape)


def _inproj(x, mod, norm_w, w, tb, tl):
    b, l, d = x.shape
    n = w.shape[1]
    tn = INPROJ_TN
    return pl.pallas_call(
        _inproj_kernel,
        grid=(b // tb, l // tl, n // tn),
        in_specs=[pl.BlockSpec((tb, tl, d), lambda i, t, j: (i, t, 0)),
                  pl.BlockSpec((tb, N_MOD, d), lambda i, t, j: (i, 0, 0)),
                  pl.BlockSpec((1, d), lambda i, t, j: (0, 0)),
                  pl.BlockSpec((d, tn), lambda i, t, j: (0, j))],
        out_specs=pl.BlockSpec((tb, tl, tn), lambda i, t, j: (i, t, j)),
        out_shape=jax.ShapeDtypeStruct((b, l, n), F32),
        scratch_shapes=[pltpu.VMEM((tb * tl, d), BF16)],
        compiler_params=_params("arbitrary", "arbitrary", "arbitrary"),
        name="in_projection",
    )(x, mod, norm_w.reshape(1, d), w)


def _unit_lower_inverse(a_list, same_block, eye):
    sizes = sorted(same_block)
    p = [-jnp.where(same_block[sizes[0]], a, 0.0) for a in a_list]
    x = [eye + pi for pi in p]
    k = 2
    while k < sizes[0]:
        p = [_dot(pi, pi) for pi in p]
        x = [xi + _dot(xi, pi) for xi, pi in zip(x, p)]
        k *= 2
    for s_prev, s in zip(sizes[:-1], sizes[1:]):
        off = jnp.logical_and(same_block[s], jnp.logical_not(same_block[s_prev]))
        xa = [_dot(xi, jnp.where(off, a, 0.0)) for xi, a in zip(x, a_list)]
        x = [xi - _dot(xai, xi) for xi, xai in zip(x, xa)]
    return x


def _gdn_kernel(qkv_ref, z_ref, ab_ref, cst_ref, s0_ref, cw_ref, hp_ref, nw_ref,
                o_ref, sout_ref, ext_ref, s_ref):
    nb, c, _ = qkv_ref.shape
    r = nb * c
    t = pl.program_id(1)

    @pl.when(t == 0)
    def _():
        ext_ref[:, 0:SUBLANES, :] = cst_ref[...]
        s_ref[...] = s0_ref[...]

    ext_ref[:, SUBLANES:SUBLANES + c, :] = qkv_ref[...]

    ab = ab_ref[...].reshape(r, LANES)
    hp = hp_ref[...]
    g = -jnp.exp(hp[0:1, :]) * _softplus(ab + hp[1:2, :])
    beta = jax.nn.sigmoid(ab)
    pos = jnp.bitwise_and(lax.broadcasted_iota(jnp.int32, (r, LANES), 0), c - 1)
    gc = g
    s = 1
    while s < c:
        gc = gc + jnp.where(pos >= s, pltpu.roll(gc, s, 0), 0.0)
        s *= 2
    if nb == 1:
        gcl = jnp.broadcast_to(gc[r - 1:r, :], (r, LANES))
    else:
        gcl = jnp.where(pos == c - 1, gc, 0.0)
        s = 1
        while s < c:
            gcl = gcl + jnp.where(pos < c - s, pltpu.roll(gcl, r - s, 0), 0.0)
            s *= 2
    eg = jnp.exp(gc)
    egl = jnp.exp(gcl - gc)
    gl_rows = [jnp.exp(gc[sgi * c + c - 1:sgi * c + c, :]) for sgi in range(nb)]
    gct = gc.T

    row = lax.broadcasted_iota(jnp.int32, (r, r), 0)
    col = lax.broadcasted_iota(jnp.int32, (r, r), 1)

    def same(sz):
        sh = int(math.log2(sz))
        return jnp.right_shift(row, sh) == jnp.right_shift(col, sh)

    sizes = []
    sz = min(INV_BASE, c)
    while sz <= c:
        sizes.append(sz)
        sz *= 2
    same_block = {sz: same(sz) for sz in sizes}
    seg = same_block[c]
    incl = jnp.logical_and(seg, row >= col)
    strict = jnp.logical_and(seg, row > col)
    eye = (row == col).astype(F32)
    colseg = jnp.right_shift(lax.broadcasted_iota(jnp.int32, (HEAD_DIM, r), 1), int(math.log2(c)))

    cw = cw_ref[...]

    def conv_cols(c0):
        cols = slice(c0, c0 + HEAD_DIM)
        if nb == 1:
            full = ext_ref[0, :, cols]
            acc = cw[GDN_CONV - 1:GDN_CONV, cols] * full[SUBLANES:]
            for i in range(GDN_CONV - 1):
                acc = acc + cw[i:i + 1, cols] * pltpu.roll(full, GDN_CONV - 1 - i, 0)[SUBLANES:]
            return _silu(acc)
        acc = cw[0:1, cols] * ext_ref[:, SUBLANES - 3:SUBLANES - 3 + c, cols]
        for i in range(1, GDN_CONV):
            lo = SUBLANES - 3 + i
            acc = acc + cw[i:i + 1, cols] * ext_ref[:, lo:lo + c, cols]
        return _silu(acc).reshape(r, HEAD_DIM)

    nw = nw_ref[...]
    heads = range(GDN_HEADS)
    q = [conv_cols(h * HEAD_DIM) for h in heads]
    k = [conv_cols(GDN_WIDTH + h * HEAD_DIM) for h in heads]
    v = [conv_cols(2 * GDN_WIDTH + h * HEAD_DIM) for h in heads]
    qn = [x * (lax.rsqrt(jnp.sum(x * x, axis=-1, keepdims=True) + EPS) * (HEAD_DIM ** -0.5)) for x in q]
    kn = [x * lax.rsqrt(jnp.sum(x * x, axis=-1, keepdims=True) + EPS) for x in k]
    bcol = [beta[:, SUBLANES + h:SUBLANES + h + 1] for h in heads]
    egcol = [eg[:, h:h + 1] for h in heads]
    dec = [jnp.exp(jnp.where(incl, gc[:, h:h + 1] - gct[h:h + 1, :], -jnp.inf)) for h in heads]
    kb = [kn[h] * bcol[h] for h in heads]
    kn16 = [x.astype(BF16) for x in kn]
    kq = [_dot_nt(jnp.concatenate([kb[h].astype(BF16), qn[h].astype(BF16)], axis=0), kn16[h]) for h in heads]
    a = [jnp.where(strict, kq[h][:r] * dec[h], 0.0) for h in heads]
    qk16 = [(kq[h][r:] * dec[h]).astype(BF16) for h in heads]
    tinv = _unit_lower_inverse(a, same_block, eye)
    sol = [_dot(tinv[h], jnp.concatenate([v[h] * bcol[h], kb[h] * egcol[h]], axis=1)) for h in heads]
    qg = [qn[h] * egcol[h] for h in heads]
    kdt16 = [(kn[h] * egl[:, h:h + 1]).T.astype(BF16) for h in heads]

    ws, qs = [], []
    for h in heads:
        w = sol[h][:, HEAD_DIM:]
        ws_parts, qs_parts = [], []
        for sgi in range(nb):
            lo = sgi * c
            wq = jnp.concatenate([w[lo:lo + c], qg[h][lo:lo + c]], axis=0)
            res = _dot(wq, s_ref[sgi, h])
            ws_parts.append(res[:c])
            qs_parts.append(res[c:])
        ws.append(ws_parts[0] if nb == 1 else jnp.concatenate(ws_parts, axis=0))
        qs.append(qs_parts[0] if nb == 1 else jnp.concatenate(qs_parts, axis=0))
    v16 = [(sol[h][:, :HEAD_DIM] - ws[h]).astype(BF16) for h in heads]
    o = [qs[h] + jnp.dot(qk16[h], v16[h], preferred_element_type=F32) for h in heads]
    for h in heads:
        for sgi in range(nb):
            if nb == 1:
                kdm = kdt16[h]
            else:
                kdm = jnp.where(colseg == sgi, kdt16[h], jnp.zeros_like(kdt16[h]))
            s_ref[sgi, h] = (s_ref[sgi, h] * gl_rows[sgi][:, h:h + 1]
                             + jnp.dot(kdm, v16[h], preferred_element_type=F32))
    for h in heads:
        on = o[h] * lax.rsqrt(jnp.mean(o[h] * o[h], axis=-1, keepdims=True) + EPS) * nw
        zh = z_ref[:, :, h * HEAD_DIM:(h + 1) * HEAD_DIM].reshape(r, HEAD_DIM)
        out = on * _silu(zh)
        o_ref[:, :, h * HEAD_DIM:(h + 1) * HEAD_DIM] = out.reshape(nb, c, HEAD_DIM).astype(o_ref.dtype)

    ext_ref[:, 0:SUBLANES, :] = ext_ref[:, c:c + SUBLANES, :]

    @pl.when(t == pl.num_programs(1) - 1)
    def _():
        sout_ref[...] = s_ref[...]


def _gdn(proj, conv_state8, s0, conv_w8, head_params, norm_w, nb, c):
    b, l, _ = proj.shape
    nt = l // c
    zb = COL_Z // GDN_WIDTH
    abb = COL_AB // LANES
    return pl.pallas_call(
        _gdn_kernel,
        grid=(b // nb, nt),
        in_specs=[pl.BlockSpec((nb, c, QKV_D_WIDTH), lambda i, t: (i, t, 0)),
                  pl.BlockSpec((nb, c, GDN_WIDTH), lambda i, t: (i, t, zb)),
                  pl.BlockSpec((nb, c, LANES), lambda i, t: (i, t, abb)),
                  pl.BlockSpec((nb, SUBLANES, QKV_D_WIDTH), lambda i, t: (i, 0, 0)),
                  pl.BlockSpec((nb, GDN_HEADS, HEAD_DIM, HEAD_DIM), lambda i, t: (i, 0, 0, 0)),
                  pl.BlockSpec((SUBLANES, QKV_D_WIDTH), lambda i, t: (0, 0)),
                  pl.BlockSpec((SUBLANES, LANES), lambda i, t: (0, 0)),
                  pl.BlockSpec((1, HEAD_DIM), lambda i, t: (0, 0))],
        out_specs=[pl.BlockSpec((nb, c, GDN_WIDTH), lambda i, t: (i, t, 0)),
                   pl.BlockSpec((nb, GDN_HEADS, HEAD_DIM, HEAD_DIM), lambda i, t: (i, 0, 0, 0))],
        out_shape=[jax.ShapeDtypeStruct((b, l, GDN_WIDTH), BF16),
                   jax.ShapeDtypeStruct((b, GDN_HEADS, HEAD_DIM, HEAD_DIM), F32)],
        scratch_shapes=[pltpu.VMEM((nb, SUBLANES + c, QKV_D_WIDTH), F32),
                        pltpu.VMEM((nb, GDN_HEADS, HEAD_DIM, HEAD_DIM), F32)],
        compiler_params=_params("arbitrary", "arbitrary"),
        name="gated_delta_rule",
    )(proj, proj, proj, conv_state8, s0, conv_w8, head_params, norm_w.reshape(1, HEAD_DIM))


def _head_rms(x, w):
    return x * lax.rsqrt(jnp.mean(x * x, axis=-1, keepdims=True) + EPS) * w


def _swa_kernel(sink_ref, q_ref, k_ref, v_ref, kp_ref, vp_ref, qw_ref, kw_ref, o_ref, kn_ref,
                *, chunk, pos0, norm_prev):
    tq = q_ref.shape[1]
    tk = max(tq, LANES)
    m = SWA_GROUP * tq
    t0 = pos0 + pl.program_id(1) * tq
    qw = qw_ref[...]
    kw = kw_ref[...]

    rown = lax.broadcasted_iota(jnp.int32, (m, 1), 0)
    grp = rown // tq
    qi = rown - grp * tq
    cs = (qi // chunk) * chunk
    kj = lax.broadcasted_iota(jnp.int32, (1, tk), 1)
    pj = lax.broadcasted_iota(jnp.int32, (1, WINDOW), 1)
    vis_own = jnp.logical_and(kj < cs + chunk, kj < tq)
    vis_prev = jnp.logical_and(pj - WINDOW >= cs - WINDOW, t0 - WINDOW + pj >= 0)
    dist_own = jnp.abs(qi - kj).astype(F32)
    dist_prev = (qi + WINDOW - pj).astype(F32)
    scale = HEAD_DIM ** -0.5

    for kvh in range(SWA_KV_HEADS):
        slope = jnp.zeros((m, 1), F32)
        sink = jnp.zeros((m, 1), F32)
        for gi in range(SWA_GROUP):
            hq = kvh * SWA_GROUP + gi
            slope = jnp.where(grp == gi, 2.0 ** (-8.0 * (hq + 1) / SWA_HEADS), slope)
            sink = jnp.where(grp == gi, sink_ref[hq], sink)
        qs = jnp.concatenate(
            [_head_rms(q_ref[0, :, (kvh * SWA_GROUP + gi) * HEAD_DIM:(kvh * SWA_GROUP + gi + 1) * HEAD_DIM], qw)
             for gi in range(SWA_GROUP)], axis=0)
        ksl = slice(kvh * HEAD_DIM, (kvh + 1) * HEAD_DIM)
        k_own = _head_rms(k_ref[0, :, ksl], kw)
        kn_ref[0, :, ksl] = k_own
        v_own = v_ref[0, :, ksl]
        if tk > tq:
            pad = jnp.zeros((tk - tq, HEAD_DIM), F32)
            k_own = jnp.concatenate([k_own, pad], axis=0)
            v_own = jnp.concatenate([v_own, pad], axis=0)
        k_prev = kp_ref[0, :, ksl]
        if norm_prev:
            k_prev = _head_rms(k_prev, kw)
        v_prev = vp_ref[0, :, ksl]

        s_own = jnp.where(vis_own, _dot_nt(qs, k_own) * scale - slope * dist_own, -jnp.inf)
        s_prev = jnp.where(vis_prev, _dot_nt(qs, k_prev) * scale - slope * dist_prev, -jnp.inf)
        mx = jnp.maximum(jnp.maximum(jnp.max(s_own, axis=-1, keepdims=True),
                                     jnp.max(s_prev, axis=-1, keepdims=True)), sink)
        p_own = jnp.exp(s_own - mx)
        p_prev = jnp.exp(s_prev - mx)
        den = (jnp.sum(p_own, axis=-1, keepdims=True) + jnp.sum(p_prev, axis=-1, keepdims=True)
               + jnp.exp(sink - mx))
        inv = 1.0 / den
        o = _dot(p_own * inv, v_own) + _dot(p_prev * inv, v_prev)
        for gi in range(SWA_GROUP):
            hq = kvh * SWA_GROUP + gi
            o_ref[0, :, hq * HEAD_DIM:(hq + 1) * HEAD_DIM] = o[gi * tq:(gi + 1) * tq].astype(o_ref.dtype)


def _swa(proj, k_prev_src, v_prev_src, prev_from_proj, q_norm_w, k_norm_w, sinks, tq, chunk, pos0):
    b, l, _ = proj.shape
    qb = COL_QA // SWA_WIDTH
    kb = COL_KA // KV_WIDTH
    vb = COL_VA // KV_WIDTH
    if prev_from_proj:
        assert tq == WINDOW
        kp_spec = pl.BlockSpec((1, WINDOW, KV_WIDTH), lambda i, t: (i, jnp.maximum(t - 1, 0), kb))
        vp_spec = pl.BlockSpec((1, WINDOW, KV_WIDTH), lambda i, t: (i, jnp.maximum(t - 1, 0), vb))
    else:
        assert l == tq
        kp_spec = pl.BlockSpec((1, WINDOW, KV_WIDTH), lambda i, t: (i, 0, 0))
        vp_spec = pl.BlockSpec((1, WINDOW, KV_WIDTH), lambda i, t: (i, 0, 0))
    kernel = functools.partial(_swa_kernel, chunk=chunk, pos0=pos0, norm_prev=prev_from_proj)
    return pl.pallas_call(
        kernel,
        grid=(b, l // tq),
        in_specs=[pl.BlockSpec(memory_space=pltpu.SMEM),
                  pl.BlockSpec((1, tq, SWA_WIDTH), lambda i, t: (i, t, qb)),
                  pl.BlockSpec((1, tq, KV_WIDTH), lambda i, t: (i, t, kb)),
                  pl.BlockSpec((1, tq, KV_WIDTH), lambda i, t: (i, t, vb)),
                  kp_spec, vp_spec,
                  pl.BlockSpec((1, HEAD_DIM), lambda i, t: (0, 0)),
                  pl.BlockSpec((1, HEAD_DIM), lambda i, t: (0, 0))],
        out_specs=[pl.BlockSpec((1, tq, SWA_WIDTH), lambda i, t: (i, t, 0)),
                   pl.BlockSpec((1, tq, KV_WIDTH), lambda i, t: (i, t, 0))],
        out_shape=[jax.ShapeDtypeStruct((b, l, SWA_WIDTH), BF16),
                   jax.ShapeDtypeStruct((b, l, KV_WIDTH), F32)],
        compiler_params=_params("arbitrary", "arbitrary"),
        name="sliding_window_attention",
    )(sinks, proj, proj, proj, k_prev_src, v_prev_src,
      q_norm_w.reshape(1, HEAD_DIM), k_norm_w.reshape(1, HEAD_DIM))


def _outproj_kernel(x_ref, od_ref, oa_ref, mod_ref, w_ref, o_ref):
    tb, tl, d = x_ref.shape
    od = od_ref[...].reshape(tb * tl, GDN_WIDTH)
    oa = oa_ref[...].reshape(tb * tl, SWA_WIDTH)
    acc = (jnp.dot(od, w_ref[0:GDN_WIDTH, :], preferred_element_type=F32)
           + jnp.dot(oa, w_ref[GDN_WIDTH:GDN_WIDTH + SWA_WIDTH, :], preferred_element_type=F32))
    o_ref[...] = x_ref[...] + mod_ref[:, 2:3, :] * acc.reshape(tb, tl, d)


def _outproj(x, o_d, o_a, mod, w_o, tb, tl):
    b, l, d = x.shape
    return pl.pallas_call(
        _outproj_kernel,
        grid=(b // tb, l // tl),
        in_specs=[pl.BlockSpec((tb, tl, d), lambda i, t: (i, t, 0)),
                  pl.BlockSpec((tb, tl, GDN_WIDTH), lambda i, t: (i, t, 0)),
                  pl.BlockSpec((tb, tl, SWA_WIDTH), lambda i, t: (i, t, 0)),
                  pl.BlockSpec((tb, N_MOD, d), lambda i, t: (i, 0, 0)),
                  pl.BlockSpec(w_o.shape, lambda i, t: (0, 0))],
        out_specs=pl.BlockSpec((tb, tl, d), lambda i, t: (i, t, 0)),
        out_shape=jax.ShapeDtypeStruct((b, l, d), F32),
        compiler_params=_params("arbitrary", "arbitrary"),
        name="out_projection",
    )(x, o_d, o_a, mod, w_o)


def _ffn_kernel(x_ref, mod_ref, nw_ref, wg_ref, wu_ref, cwg_ref, cwu_ref, bg_ref, bu_ref,
                stg_ref, stu_ref, wd_ref, y_ref, lastg_ref, lastu_ref,
                h_ref, act_ref, extg_ref, extu_ref, carg_ref, caru_ref):
    tb, tl, d = x_ref.shape
    tf = wg_ref.shape[1]
    t = pl.program_id(1)
    j = pl.program_id(2)

    halves = ((wg_ref, cwg_ref, bg_ref, stg_ref, extg_ref, carg_ref, lastg_ref),
              (wu_ref, cwu_ref, bu_ref, stu_ref, extu_ref, caru_ref, lastu_ref))

    @pl.when(j == 0)
    def _():
        h = _modulated_norm(x_ref[...], nw_ref[...], mod_ref[:, 3:4, :], mod_ref[:, 4:5, :])
        h_ref[...] = h.reshape(tb * tl, d).astype(BF16)
        y_ref[...] = jnp.zeros_like(y_ref)

    @pl.when(t == 0)
    def _():
        for _, _, _, st_ref, ext_ref, _, _ in halves:
            ext_ref[:, 0:SUBLANES, :] = st_ref[...]

    @pl.when(t > 0)
    def _():
        for _, _, _, _, ext_ref, car_ref, _ in halves:
            ext_ref[:, 0:SUBLANES, :] = car_ref[j]

    h = h_ref[...]
    ncb = tf // FFN_COLS
    for cb in range(ncb):
        cs = slice(cb * FFN_COLS, (cb + 1) * FFN_COLS)
        for w_ref, _, _, _, ext_ref, _, _ in halves:
            u = jnp.dot(h, w_ref[:, cs], preferred_element_type=F32)
            ext_ref[:, SUBLANES:SUBLANES + tl, cs] = u.reshape(tb, tl, FFN_COLS)
    for _, _, _, _, ext_ref, car_ref, last_ref in halves:
        tail = ext_ref[:, tl:tl + SUBLANES, :]
        car_ref[j] = tail
        last_ref[:, 0] = tail

    if tb == 1:
        chunks = [(slice(0, 1), r0, FFN_ROWS) for r0 in range(0, tl, FFN_ROWS)]
    else:
        nbc = max(1, FFN_ROWS // tl)
        chunks = [(slice(b0, b0 + nbc), 0, tl) for b0 in range(0, tb, nbc)]
    for cb in range(ncb):
        cs = slice(cb * FFN_COLS, (cb + 1) * FFN_COLS)
        for bs, r0, nr in chunks:
            conv = []
            for _, cw_ref, b_ref, _, ext_ref, _, _ in halves:
                acc = b_ref[:, cs]
                for i in range(FFN_CONV):
                    lo = SUBLANES - (FFN_CONV - 1) + i + r0
                    acc = acc + cw_ref[i:i + 1, cs] * ext_ref[bs, lo:lo + nr, cs]
                conv.append(acc)
            act = _silu(conv[0]) * conv[1]
            nbs = bs.stop - bs.start
            row0 = bs.start * tl + r0
            act_ref[row0:row0 + nbs * nr, cs] = act.reshape(nbs * nr, FFN_COLS).astype(BF16)

    y_ref[...] += jnp.dot(act_ref[...], wd_ref[...], preferred_element_type=F32).reshape(tb, tl, d)

    @pl.when(j == pl.num_programs(2) - 1)
    def _():
        y_ref[...] = x_ref[...] + mod_ref[:, 5:6, :] * y_ref[...]


def _ffn(x, mod, norm_w, w_up, conv_w8, conv_b, state8, w_down, tb, tl, tf):
    b, l, d = x.shape
    d_ff = w_down.shape[0]
    nj = d_ff // tf
    cb = conv_b.reshape(1, 2 * d_ff)
    lo = lambda i, t, j: (0, j)
    hi = lambda i, t, j: (0, nj + j)
    return pl.pallas_call(
        _ffn_kernel,
        grid=(b // tb, l // tl, nj),
        in_specs=[pl.BlockSpec((tb, tl, d), lambda i, t, j: (i, t, 0), pipeline_mode=pl.Buffered(1)),
                  pl.BlockSpec((tb, N_MOD, d), lambda i, t, j: (i, 0, 0)),
                  pl.BlockSpec((1, d), lambda i, t, j: (0, 0)),
                  pl.BlockSpec((d, tf), lo), pl.BlockSpec((d, tf), hi),
                  pl.BlockSpec((SUBLANES, tf), lo), pl.BlockSpec((SUBLANES, tf), hi),
                  pl.BlockSpec((1, tf), lo), pl.BlockSpec((1, tf), hi),
                  pl.BlockSpec((tb, SUBLANES, tf), lambda i, t, j: (i, 0, j)),
                  pl.BlockSpec((tb, SUBLANES, tf), lambda i, t, j: (i, 0, nj + j)),
                  pl.BlockSpec((tf, d), lambda i, t, j: (j, 0))],
        out_specs=[pl.BlockSpec((tb, tl, d), lambda i, t, j: (i, t, 0)),
                   pl.BlockSpec((tb, 1, SUBLANES, tf), lambda i, t, j: (i, t, 0, j)),
                   pl.BlockSpec((tb, 1, SUBLANES, tf), lambda i, t, j: (i, t, 0, j))],
        out_shape=[jax.ShapeDtypeStruct((b, l, d), F32),
                   jax.ShapeDtypeStruct((b, l // tl, SUBLANES, d_ff), F32),
                   jax.ShapeDtypeStruct((b, l // tl, SUBLANES, d_ff), F32)],
        scratch_shapes=[pltpu.VMEM((tb * tl, d), BF16),
                        pltpu.VMEM((tb * tl, tf), BF16),
                        pltpu.VMEM((tb, SUBLANES + tl, tf), F32),
                        pltpu.VMEM((tb, SUBLANES + tl, tf), F32),
                        pltpu.VMEM((nj, tb, SUBLANES, tf), F32),
                        pltpu.VMEM((nj, tb, SUBLANES, tf), F32)],
        compiler_params=_params("arbitrary", "arbitrary", "arbitrary"),
        name="conv_ffn",
    )(x, mod, norm_w.reshape(1, d), w_up, w_up, conv_w8, conv_w8, cb, cb, state8, state8, w_down)


def _pad_rows_front(x, rows):
    pad = rows - x.shape[1]
    return jnp.pad(x, ((0, 0), (pad, 0), (0, 0)))


def _layer(x, mod, conv_prev, s0, k_prev, v_prev, ffn_prev, pos0, wts, cfg):
    b, l, d = x.shape
    proj = _inproj(x, mod, wts["norm1_w"], wts["w_in"], cfg["tb"], cfg["tl_in"])

    nb = cfg["gdn_nb"]
    c = GDN_ROWS // nb
    o_d, s_new = _gdn(proj, _pad_rows_front(conv_prev, SUBLANES), s0, wts["conv_qkv_w8"],
                      wts["head_params"], wts["gdn_norm_w"], nb, c)

    chunk = min(CHUNK, l)
    if k_prev is None:
        o_a, kn = _swa(proj, proj, proj, True, wts["q_norm_w"], wts["k_norm_w"], wts["sinks"],
                       cfg["tq"], chunk, pos0)
    else:
        o_a, kn = _swa(proj, k_prev.reshape(b, WINDOW, KV_WIDTH), v_prev.reshape(b, WINDOW, KV_WIDTH), False,
                       wts["q_norm_w"], wts["k_norm_w"], wts["sinks"], cfg["tq"], chunk, pos0)

    x1 = _outproj(x, o_d, o_a, mod, wts["w_o"], cfg["tb"], cfg["tl_out"])
    y, last_g, last_u = _ffn(x1, mod, wts["norm2_w"], wts["w_up"], wts["ffn_conv_w8"], wts["ffn_conv_b"],
                             _pad_rows_front(ffn_prev, SUBLANES), wts["w_down"],
                             cfg["tb"], cfg["tl_ffn"], cfg["tf"])

    qkv_new = proj[:, l - (GDN_CONV - 1):, COL_QKV:COL_QKV + QKV_D_WIDTH]
    if l >= GDN_CONV - 1:
        conv_new = qkv_new
    else:
        conv_new = jnp.concatenate([conv_prev, qkv_new], axis=1)[:, -(GDN_CONV - 1):]
    if l >= WINDOW:
        k_cache = kn[:, l - WINDOW:]
        v_cache = proj[:, l - WINDOW:, COL_VA:COL_VA + KV_WIDTH]
    else:
        v_new = proj[:, :, COL_VA:COL_VA + KV_WIDTH]
        k_cache = jnp.concatenate([k_prev.reshape(b, WINDOW, KV_WIDTH)[:, l:], kn], axis=1)
        v_cache = jnp.concatenate([v_prev.reshape(b, WINDOW, KV_WIDTH)[:, l:], v_new], axis=1)
    k_cache = k_cache.reshape(b, WINDOW, SWA_KV_HEADS, HEAD_DIM)
    v_cache = v_cache.reshape(b, WINDOW, SWA_KV_HEADS, HEAD_DIM)
    ffn_new = jnp.concatenate([last_g[:, -1, SUBLANES - (FFN_CONV - 1):],
                               last_u[:, -1, SUBLANES - (FFN_CONV - 1):]], axis=-1)
    return y, conv_new, s_new, k_cache, v_cache, ffn_new


def _prep_weights(l, ada_w, ada_b, norm1_w, norm2_w, w_in, conv_qkv_w, a_log, dt_bias, gdn_norm_w,
                  q_norm_w, k_norm_w, sinks, w_o, w_up, ffn_conv_w, ffn_conv_b, w_down):
    d = w_in.shape[1]
    ab0 = QKV_D_WIDTH + GDN_WIDTH
    ab1 = ab0 + 2 * GDN_HEADS
    w = w_in[l]
    w_in_r = jnp.concatenate(
        [w[:, :ab0], w[:, ab1:], w[:, ab0:ab1], jnp.zeros((d, LANES - 2 * GDN_HEADS), w.dtype)], axis=1).astype(BF16)
    assert w_in_r.shape[1] == PROJ_WIDTH
    hp = jnp.zeros((SUBLANES, LANES), F32)
    hp = hp.at[0, :GDN_HEADS].set(a_log[l].astype(F32)).at[1, :GDN_HEADS].set(dt_bias[l].astype(F32))
    return {
        "ada_w": ada_w[l], "ada_b": ada_b[l], "norm1_w": norm1_w[l], "norm2_w": norm2_w[l],
        "w_in": w_in_r,
        "conv_qkv_w8": jnp.pad(conv_qkv_w[l], ((0, SUBLANES - GDN_CONV), (0, 0))),
        "head_params": hp,
        "gdn_norm_w": gdn_norm_w[l], "q_norm_w": q_norm_w[l], "k_norm_w": k_norm_w[l], "sinks": sinks[l],
        "w_o": w_o[l].astype(BF16), "w_up": w_up[l].astype(BF16),
        "ffn_conv_w8": jnp.pad(ffn_conv_w[l], ((0, SUBLANES - FFN_CONV), (0, 0))),
        "ffn_conv_b": ffn_conv_b[l], "w_down": w_down[l].astype(BF16),
    }


PROMPT_CFG = dict(tb=1, tl_in=1024, gdn_nb=1, tq=128, tl_out=512, tl_ffn=1024, tf=FFN_TF)


def _sample_cfg(b, l):
    return dict(tb=b, tl_in=l, gdn_nb=GDN_ROWS // l, tq=l, tl_out=l, tl_ffn=l, tf=FFN_TF)


def kernel(x_prompt, x_sample, state_conv_qkv, state_delta, cache_swa_k, cache_swa_v, state_ffn_conv,
           c_prompt, c_sample, ada_w, ada_b, norm1_w, norm2_w, w_in, conv_qkv_w, a_log, dt_bias,
           gdn_norm_w, q_norm_w, k_norm_w, sinks, w_o, w_up, ffn_conv_w, ffn_conv_b, w_down):
    depth = w_in.shape[0]
    bp, lp, d = x_prompt.shape
    bs, ls, _ = x_sample.shape
    d_ff = w_down.shape[1]
    xp, xs = x_prompt, x_sample
    acc_p = [[] for _ in range(5)]
    acc_s = [[] for _ in range(5)]
    c_all = jnp.concatenate([c_prompt, c_sample], axis=0)
    c_rows = -(-c_all.shape[0] // SUBLANES) * SUBLANES
    c_all = jnp.pad(c_all, ((0, c_rows - c_all.shape[0]), (0, 0)))
    for l in range(depth):
        wts = _prep_weights(l, ada_w, ada_b, norm1_w, norm2_w, w_in, conv_qkv_w, a_log, dt_bias, gdn_norm_w,
                            q_norm_w, k_norm_w, sinks, w_o, w_up, ffn_conv_w, ffn_conv_b, w_down)
        mod = _modulation(c_all, wts["ada_w"], wts["ada_b"]).reshape(c_rows, N_MOD, d)
        mod_p, mod_s = mod[:bp], mod[bp:bp + bs]
        dt = xp.dtype
        zero_state = (jnp.zeros((bp, GDN_CONV - 1, QKV_D_WIDTH), dt),
                      jnp.zeros((bp, GDN_HEADS, HEAD_DIM, HEAD_DIM), dt),
                      None, None,
                      jnp.zeros((bp, FFN_CONV - 1, 2 * d_ff), dt))
        xp, *new_p = _layer(xp, mod_p, *zero_state, 0, wts, PROMPT_CFG)
        xs, *new_s = _layer(xs, mod_s, state_conv_qkv[l], state_delta[l], cache_swa_k[l], cache_swa_v[l],
                            state_ffn_conv[l], PAST_LEN, wts, _sample_cfg(bs, ls))
        for acc, t in zip(acc_p, new_p):
            acc.append(t)
        for acc, t in zip(acc_s, new_s):
            acc.append(t)
    outs_p = [jnp.stack(a) for a in acc_p]
    outs_s = [jnp.stack(a) for a in acc_s]
    return (xp, xs, *outs_p, *outs_s)
```
